```python
import math
import jax, jax.numpy as jnp
from jax import lax
import numpy as np

D_MODEL = 1024
BATCH = 8
SEQ = 2048
DEPTH = 1

N_MEM = 256
GDN_HEADS = 8
GDN_DK = 128
GDN_DV = 128
GDN_KW = GDN_HEADS * GDN_DK
GDN_VW = GDN_HEADS * GDN_DV
SHORT_CONV = 4
CHUNK = 64
CONV_CH = D_MODEL
CONV_WIDTH = 31
XATTN_HEADS = 4
XATTN_HD = D_MODEL // XATTN_HEADS
D_FF = 2816
QKV_W = 2 * GDN_KW + GDN_VW
IN_SIZES = (QKV_W, GDN_VW, GDN_HEADS, GDN_HEADS, 2 * CONV_CH, 2 * D_MODEL)
IN_WIDTH = QKV_W + GDN_VW + 2 * GDN_HEADS + 2 * CONV_CH + 2 * D_MODEL
DN_ALPHA = (2 * DEPTH) ** 0.25
DN_BETA = (8 * DEPTH) ** -0.25
LN_EPS = 1e-5
RMS_EPS = 1e-6

kernel_name = "hybrid_gdn_conformer_deepnorm_block"


def layer_norm(x, g, b):
    xf = x.astype(jnp.float32)
    mu = jnp.mean(xf, axis=-1, keepdims=True)
    var = jnp.mean(jnp.square(xf - mu), axis=-1, keepdims=True)
    y = (xf - mu) * lax.rsqrt(var + LN_EPS)
    return (y * g.astype(jnp.float32) + b.astype(jnp.float32)).astype(x.dtype)


def rms_norm(x, g):
    xf = x.astype(jnp.float32)
    y = xf * lax.rsqrt(jnp.mean(jnp.square(xf), axis=-1, keepdims=True) + RMS_EPS)
    return (y * g.astype(jnp.float32)).astype(x.dtype)


def l2_normalize(x):
    return x * lax.rsqrt(jnp.sum(jnp.square(x), axis=-1, keepdims=True) + RMS_EPS)


def causal_depthwise_conv(x, w):
    k = w.shape[0]
    return lax.conv_general_dilated(
        x, w[:, None, :].astype(x.dtype), window_strides=(1,), padding=[(k - 1, 0)],
        dimension_numbers=("NWC", "WIO", "NWC"), feature_group_count=x.shape[-1])


def swiglu_ffn(x, w_gate, w_up, w_down):
    return (jax.nn.silu(x @ w_gate) * (x @ w_up)) @ w_down


def chunk_gated_delta_rule(q, k, v, g, beta):
    b, l, h, dk = q.shape
    dv = v.shape[-1]
    n = l // CHUNK
    q = l2_normalize(q) * (dk ** -0.5)
    k = l2_normalize(k)

    def chunks(t):
        return t.reshape(b, n, CHUNK, h, -1).transpose(0, 3, 1, 2, 4)

    qc, kc, vc = chunks(q), chunks(k), chunks(v)
    gc = jnp.cumsum(g.reshape(b, n, CHUNK, h).transpose(0, 3, 1, 2), axis=-1)
    bc = beta.reshape(b, n, CHUNK, h).transpose(0, 3, 1, 2)
    kb = kc * bc[..., None]
    vb = vc * bc[..., None]

    tri = jnp.tril(jnp.ones((CHUNK, CHUNK), dtype=bool))
    strict = jnp.tril(jnp.ones((CHUNK, CHUNK), dtype=bool), -1)
    diff = gc[..., :, None] - gc[..., None, :]
    decay = jnp.where(tri, jnp.exp(jnp.where(tri, diff, 0.0)), 0.0)

    m = jnp.where(strict, jnp.einsum("bhncd,bhnmd->bhncm", kb, kc) * decay, 0.0)
    t_sys = jnp.eye(CHUNK, dtype=q.dtype) + m
    u = lax.linalg.triangular_solve(t_sys, vb, left_side=True, lower=True, unit_diagonal=True)
    w = lax.linalg.triangular_solve(t_sys, kb * jnp.exp(gc)[..., None], left_side=True,
                                    lower=True, unit_diagonal=True)
    a_qk = jnp.einsum("bhncd,bhnmd->bhncm", qc, kc) * decay

    def step(state, xs):
        q_i, k_i, u_i, w_i, a_i, g_i = xs
        v_new = u_i - jnp.einsum("bhcd,bhde->bhce", w_i, state)
        o_i = (jnp.einsum("bhcd,bhde->bhce", q_i * jnp.exp(g_i)[..., None], state)
               + jnp.einsum("bhcm,bhme->bhce", a_i, v_new))
        g_last = g_i[..., -1]
        k_dec = k_i * jnp.exp(g_last[..., None] - g_i)[..., None]
        state = state * jnp.exp(g_last)[..., None, None] + jnp.einsum("bhcd,bhce->bhde", k_dec, v_new)
        return state, o_i

    xs = tuple(jnp.moveaxis(t, 2, 0) for t in (qc, kc, u, w, a_qk, gc))
    s0 = jnp.zeros((b, h, dk, dv), dtype=q.dtype)
    _, o = lax.scan(step, s0, xs)
    return o.transpose(1, 0, 3, 2, 4).reshape(b, l, h, dv)


def parallel_mixers(h, w_in, gdn_conv_qkv, gdn_a_log, gdn_dt_bias, gdn_norm_g, w_gdn_out,
                    conv_dw_w, conv_dw_b, conv_ln_g, conv_ln_b, w_conv_out, b_conv_out, w_mix_out):
    b, l, _ = h.shape
    proj = h @ w_in
    offs = np.cumsum(IN_SIZES)[:-1].tolist()
    qkv, z, a, bt, glu, gates = jnp.split(proj, offs, axis=-1)

    qkv = jax.nn.silu(causal_depthwise_conv(qkv, gdn_conv_qkv))
    q, k, v = jnp.split(qkv, [GDN_KW, 2 * GDN_KW], axis=-1)
    q = q.reshape(b, l, GDN_HEADS, GDN_DK)
    k = k.reshape(b, l, GDN_HEADS, GDN_DK)
    v = v.reshape(b, l, GDN_HEADS, GDN_DV)
    af = a.astype(jnp.float32)
    g = -jnp.exp(gdn_a_log.astype(jnp.float32)) * jax.nn.softplus(af + gdn_dt_bias.astype(jnp.float32))
    beta = jax.nn.sigmoid(bt.astype(jnp.float32))
    o = chunk_gated_delta_rule(q.astype(jnp.float32), k.astype(jnp.float32),
                               v.astype(jnp.float32), g, beta).astype(h.dtype)
    o = rms_norm(o, gdn_norm_g) * jax.nn.silu(z.reshape(b, l, GDN_HEADS, GDN_DV))
    y_a = o.reshape(b, l, GDN_VW) @ w_gdn_out

    c_lin, c_gate = jnp.split(glu, 2, axis=-1)
    c = c_lin * jax.nn.sigmoid(c_gate)
    c = causal_depthwise_conv(c, conv_dw_w) + conv_dw_b
    c = jax.nn.silu(layer_norm(c, conv_ln_g, conv_ln_b))
    y_c = c @ w_conv_out + b_conv_out

    g_a, g_c = jnp.split(gates, 2, axis=-1)
    y = jax.nn.sigmoid(g_a) * y_a + jax.nn.sigmoid(g_c) * y_c
    return y @ w_mix_out


def memory_cross_attention(x, mem, w_xq, w_xkv, w_xo):
    b, l, _ = x.shape
    q = (x @ w_xq).reshape(b, l, XATTN_HEADS, XATTN_HD)
    k, v = jnp.split(mem @ w_xkv, 2, axis=-1)
    k = k.reshape(b, -1, XATTN_HEADS, XATTN_HD)
    v = v.reshape(b, -1, XATTN_HEADS, XATTN_HD)
    s = jnp.einsum("blhd,bmhd->bhlm", q, k).astype(jnp.float32) * (XATTN_HD ** -0.5)
    p = jax.nn.softmax(s, axis=-1).astype(v.dtype)
    o = jnp.einsum("bhlm,bmhd->blhd", p, v).reshape(b, l, D_MODEL)
    return o @ w_xo


def setup_inputs(seed: int = 0) -> dict:
    key = jax.random.key(seed)
    keys = jax.random.split(key, 48)
    counter = [0]

    def nk():
        kk = keys[counter[0]]
        counter[0] += 1
        return kk

    def nrm(shape, scale):
        return jax.random.normal(nk(), shape, jnp.float32) * scale

    def gain(shape):
        return 1.0 + nrm(shape, 0.02)

    L = DEPTH
    dt = jnp.exp(jax.random.uniform(nk(), (L, GDN_HEADS), jnp.float32,
                                    minval=math.log(1e-3), maxval=math.log(0.1)))
    return {
        "x": nrm((BATCH, SEQ, D_MODEL), 1.0),
        "mem": nrm((BATCH, N_MEM, D_MODEL), 1.0),
        "ffn1_wg": nrm((L, D_MODEL, D_FF), D_MODEL ** -0.5),
        "ffn1_wu": nrm((L, D_MODEL, D_FF), D_MODEL ** -0.5),
        "ffn1_wd": nrm((L, D_FF, D_MODEL), DN_BETA * D_FF ** -0.5),
        "ln1_g": gain((L, D_MODEL)),
        "ln1_b": nrm((L, D_MODEL), 0.02),
        "w_in": nrm((L, D_MODEL, IN_WIDTH), D_MODEL ** -0.5),
        "gdn_conv_qkv": nrm((L, SHORT_CONV, QKV_W), SHORT_CONV ** -0.5),
        "gdn_a_log": jnp.log(jax.random.uniform(nk(), (L, GDN_HEADS), jnp.float32, minval=1.0, maxval=16.0)),
        "gdn_dt_bias": dt + jnp.log(-jnp.expm1(-dt)),
        "gdn_norm_g": gain((L, GDN_DV)),
        "w_gdn_out": nrm((L, GDN_VW, D_MODEL), GDN_VW ** -0.5),
        "conv_dw_w": nrm((L, CONV_WIDTH, CONV_CH), CONV_WIDTH ** -0.5),
        "conv_dw_b": nrm((L, CONV_CH), 0.02),
        "conv_ln_g": gain((L, CONV_CH)),
        "conv_ln_b": nrm((L, CONV_CH), 0.02),
        "w_conv_out": nrm((L, CONV_CH, D_MODEL), CONV_CH ** -0.5),
        "b_conv_out": nrm((L, D_MODEL), 0.02),
        "w_mix_out": nrm((L, D_MODEL, D_MODEL), DN_BETA * D_MODEL ** -0.5),
        "ln2_g": gain((L, D_MODEL)),
        "ln2_b": nrm((L, D_MODEL), 0.02),
        "w_xq": nrm((L, D_MODEL, D_MODEL), D_MODEL ** -0.5),
        "w_xkv": nrm((L, D_MODEL, 2 * D_MODEL), D_MODEL ** -0.5),
        "w_xo": nrm((L, D_MODEL, D_MODEL), DN_BETA * D_MODEL ** -0.5),
        "ln3_g": gain((L, D_MODEL)),
        "ln3_b": nrm((L, D_MODEL), 0.02),
        "ffn2_wg": nrm((L, D_MODEL, D_FF), D_MODEL ** -0.5),
        "ffn2_wu": nrm((L, D_MODEL, D_FF), D_MODEL ** -0.5),
        "ffn2_wd": nrm((L, D_FF, D_MODEL), DN_BETA * D_FF ** -0.5),
        "ln4_g": gain((L, D_MODEL)),
        "ln4_b": nrm((L, D_MODEL), 0.02),
    }


def reference(x, mem, ffn1_wg, ffn1_wu, ffn1_wd, ln1_g, ln1_b,
              w_in, gdn_conv_qkv, gdn_a_log, gdn_dt_bias, gdn_norm_g, w_gdn_out,
              conv_dw_w, conv_dw_b, conv_ln_g, conv_ln_b, w_conv_out, b_conv_out, w_mix_out,
              ln2_g, ln2_b, w_xq, w_xkv, w_xo, ln3_g, ln3_b,
              ffn2_wg, ffn2_wu, ffn2_wd, ln4_g, ln4_b):
    for i in range(DEPTH):
        x = layer_norm(DN_ALPHA * x + 0.5 * swiglu_ffn(x, ffn1_wg[i], ffn1_wu[i], ffn1_wd[i]),
                       ln1_g[i], ln1_b[i])
        y = parallel_mixers(x, w_in[i], gdn_conv_qkv[i], gdn_a_log[i], gdn_dt_bias[i], gdn_norm_g[i],
                            w_gdn_out[i], conv_dw_w[i], conv_dw_b[i], conv_ln_g[i], conv_ln_b[i],
                            w_conv_out[i], b_conv_out[i], w_mix_out[i])
        x = layer_norm(DN_ALPHA * x + y, ln2_g[i], ln2_b[i])
        x = layer_norm(DN_ALPHA * x + memory_cross_attention(x, mem, w_xq[i], w_xkv[i], w_xo[i]),
                       ln3_g[i], ln3_b[i])
        x = layer_norm(DN_ALPHA * x + 0.5 * swiglu_ffn(x, ffn2_wg[i], ffn2_wu[i], ffn2_wd[i]),
                       ln4_g[i], ln4_b[i])
    return x
```

```python
import functools

import jax
import jax.numpy as jnp
from jax import lax
from jax.experimental import pallas as pl
from jax.experimental.pallas import tpu as pltpu

F32 = jnp.float32
BF16 = jnp.bfloat16

LN_EPS = 1e-5
RMS_EPS = 1e-6
GDN_HEADS = 8
GDN_DK = 128
GDN_DV = 128
SHORT_CONV = 4
CHUNK = 64
CONV_WIDTH = 31
XATTN_HEADS = 4
N_MEM = 256

LANES = 128
SUBLANES = 8
VMEM_LIMIT = 56 * 1024 * 1024


def _params(n_axes):
    return pltpu.CompilerParams(dimension_semantics=("arbitrary",) * n_axes,
                                vmem_limit_bytes=VMEM_LIMIT)


def _const_spec(shape):
    zeros = (0,) * len(shape)
    return pl.BlockSpec(shape, lambda *_: zeros, pipeline_mode=pl.Buffered(1))


def _mm(a, b):
    return jnp.dot(a.astype(BF16), b.astype(BF16), preferred_element_type=F32)


def _mm_nt(a, b):
    return lax.dot_general(a.astype(BF16), b.astype(BF16), (((1,), (1,)), ((), ())),
                           preferred_element_type=F32)


def _mm_exact(a, b):
    return jnp.dot(a, b, preferred_element_type=F32, precision=lax.Precision.HIGHEST)


def _layer_norm(v, g, b):
    mu = jnp.mean(v, axis=-1, keepdims=True)
    d = v - mu
    var = jnp.mean(d * d, axis=-1, keepdims=True)
    return d * lax.rsqrt(var + LN_EPS) * g + b


def _silu(v):
    return v * jax.nn.sigmoid(v)


def _softplus(v):
    return jnp.maximum(v, 0.0) + jnp.log1p(jnp.exp(-jnp.abs(v)))


def _ffn_ln_kernel(x_ref, wg_ref, wu_ref, wd_ref, g_ref, b_ref, o_ref, *, alpha, fc):
    x = x_ref[...]
    xb = x.astype(BF16)
    acc = jnp.zeros(x.shape, F32)
    for j in range(wg_ref.shape[1] // fc):
        sl = slice(j * fc, (j + 1) * fc)
        gate = jnp.dot(xb, wg_ref[:, sl], preferred_element_type=F32)
        up = jnp.dot(xb, wu_ref[:, sl], preferred_element_type=F32)
        h = (_silu(gate) * up).astype(BF16)
        acc = acc + jnp.dot(h, wd_ref[sl, :], preferred_element_type=F32)
    o_ref[...] = _layer_norm(alpha * x + 0.5 * acc, g_ref[...], b_ref[...])


def _ffn_ln(x, wg, wu, wd, g, b, *, alpha, tm=512, fc=256):
    t, d = x.shape
    f = wg.shape[1]
    return pl.pallas_call(
        functools.partial(_ffn_ln_kernel, alpha=alpha, fc=fc),
        grid=(t // tm,),
        in_specs=[pl.BlockSpec((tm, d), lambda i: (i, 0)),
                  _const_spec((d, f)), _const_spec((d, f)), _const_spec((f, d)),
                  _const_spec((1, d)), _const_spec((1, d))],
        out_specs=pl.BlockSpec((tm, d), lambda i: (i, 0)),
        out_shape=jax.ShapeDtypeStruct((t, d), F32),
        compiler_params=_params(1),
        name="ffn_ln",
    )(x, wg, wu, wd, g, b)


def _in_proj_kernel(x_ref, wqkv_ref, wz_ref, wab_ref, wabt_ref, wglu_ref, wgate_ref,
                    qkv_ref, z_ref, ab_ref, arow_ref, c_ref, sg_ref):
    xb = x_ref[...].astype(BF16)
    qkv_ref[...] = jnp.dot(xb, wqkv_ref[...], preferred_element_type=F32)
    z_ref[...] = jnp.dot(xb, wz_ref[...], preferred_element_type=F32)
    ab_ref[...] = jnp.dot(xb, wab_ref[...], preferred_element_type=F32)
    arow_ref[...] = lax.dot_general(wabt_ref[...], xb, (((1,), (1,)), ((), ())),
                                    preferred_element_type=F32)
    glu = jnp.dot(xb, wglu_ref[...], preferred_element_type=F32)
    cc = glu.shape[1] // 2
    c_ref[...] = glu[:, :cc] * jax.nn.sigmoid(glu[:, cc:])
    sg_ref[...] = jax.nn.sigmoid(jnp.dot(xb, wgate_ref[...], preferred_element_type=F32))


def _in_proj(x, wqkv, wz, wab, wabt, wglu, wgate, *, tm=256):
    t, d = x.shape
    nq, nz, ng, ns = wqkv.shape[1], wz.shape[1], wglu.shape[1], wgate.shape[1]
    row = lambda i: (i, 0)
    return pl.pallas_call(
        _in_proj_kernel,
        grid=(t // tm,),
        in_specs=[pl.BlockSpec((tm, d), row),
                  _const_spec(wqkv.shape), _const_spec(wz.shape), _const_spec(wab.shape),
                  _const_spec(wabt.shape), _const_spec(wglu.shape), _const_spec(wgate.shape)],
        out_specs=[pl.BlockSpec((tm, nq), row), pl.BlockSpec((tm, nz), row),
                   pl.BlockSpec((tm, LANES), row), pl.BlockSpec((SUBLANES, tm), lambda i: (0, i)),
                   pl.BlockSpec((tm, ng // 2), row), pl.BlockSpec((tm, ns), row)],
        out_shape=[jax.ShapeDtypeStruct((t, nq), F32), jax.ShapeDtypeStruct((t, nz), F32),
                   jax.ShapeDtypeStruct((t, LANES), F32), jax.ShapeDtypeStruct((SUBLANES, t), F32),
                   jax.ShapeDtypeStruct((t, ng // 2), F32), jax.ShapeDtypeStruct((t, ns), F32)],
        compiler_params=_params(1),
        name="in_proj",
    )(x, wqkv, wz, wab, wabt, wglu, wgate)


def _unit_lower_solve(m, rhs, row, col):
    base = 16
    in_base = (row // base) == (col // base)
    n = jnp.where(in_base, -m, 0.0)
    inv = jnp.where(row == col, 1.0, 0.0) + n
    for _ in range(3):
        n = _mm(n, n)
        inv = inv + _mm(n, inv)
    s = base
    while s < CHUNK:
        lower_left = ((row // (2 * s)) == (col // (2 * s))) & ((row // s) == (col // s) + 1)
        c = jnp.where(lower_left, m, 0.0)
        inv = inv - _mm(inv, _mm(c, inv))
        s *= 2
    x0 = _mm(inv, rhs)
    resid = rhs - x0 - _mm_exact(m, x0)
    return x0 + _mm(inv, resid)


def _gdn_kernel(qkv_ref, z_ref, ab_ref, arow_ref, cw_ref, alog_r_ref, dtb_r_ref, alog_c_ref,
                dtb_c_ref, ng_ref, o_ref, ext_ref, s_ref, *, tl):
    halo = SUBLANES
    kw = GDN_HEADS * GDN_DK

    @pl.when(pl.program_id(1) == 0)
    def _():
        ext_ref[0:halo, :] = jnp.zeros((halo, ext_ref.shape[1]), F32)
        s_ref[...] = jnp.zeros(s_ref.shape, F32)

    ext_ref[halo:halo + tl, :] = qkv_ref[...]

    def conv_silu(c0):
        acc = None
        for j in range(SHORT_CONV):
            r0 = halo - (SHORT_CONV - 1) + j
            term = cw_ref[j:j + 1, c0:c0 + LANES] * ext_ref[r0:r0 + tl, c0:c0 + LANES]
            acc = term if acc is None else acc + term
        return _silu(acc)

    row = lax.broadcasted_iota(jnp.int32, (tl, tl), 0)
    col = lax.broadcasted_iota(jnp.int32, (tl, tl), 1)
    same = (row // CHUNK) == (col // CHUNK)
    tri = same & (row >= col)
    strict = same & (row > col)

    ab = ab_ref[...]
    g_col = -jnp.exp(alog_r_ref[...]) * _softplus(ab + dtb_r_ref[...])
    beta_all = jax.nn.sigmoid(ab)
    gc_col = _mm_exact(jnp.where(tri, 1.0, 0.0), g_col)
    gl_col = _mm_exact(jnp.where(same, 1.0, 0.0), g_col)
    g_row = -jnp.exp(alog_c_ref[...]) * _softplus(arow_ref[...] + dtb_c_ref[...])
    gc_row = _mm_exact(g_row, jnp.where(same & (row <= col), 1.0, 0.0))

    for h in range(GDN_HEADS):
        q = conv_silu(h * GDN_DK)
        k = conv_silu(kw + h * GDN_DK)
        v = conv_silu(2 * kw + h * GDN_DV)
        q = q * lax.rsqrt(jnp.sum(q * q, axis=-1, keepdims=True) + RMS_EPS) * (GDN_DK ** -0.5)
        k = k * lax.rsqrt(jnp.sum(k * k, axis=-1, keepdims=True) + RMS_EPS)
        beta = beta_all[:, GDN_HEADS + h:GDN_HEADS + h + 1]
        gcol = gc_col[:, h:h + 1]
        glast = gl_col[:, h:h + 1]
        grow = gc_row[h:h + 1, :]
        kb = k * beta
        vb = v * beta
        decay = jnp.where(tri, jnp.exp(jnp.where(tri, gcol - grow, 0.0)), 0.0)
        m = jnp.where(strict, _mm_nt(kb, k) * decay, 0.0)
        a = _mm_nt(q, k) * decay
        e_g = jnp.exp(gcol)
        x = _unit_lower_solve(m, jnp.concatenate([vb, kb * e_g], axis=1), row, col)
        u = x[:, :GDN_DV]
        w = x[:, GDN_DV:].astype(BF16)
        qg = (q * e_g).astype(BF16)
        kdec = k * jnp.exp(glast - gcol)
        state = s_ref[h]
        outs = []
        for c in range(tl // CHUNK):
            r = slice(c * CHUNK, (c + 1) * CHUNK)
            sb = state.astype(BF16)
            v_new = u[r] - jnp.dot(w[r], sb, preferred_element_type=F32)
            vnb = v_new.astype(BF16)
            outs.append(jnp.dot(qg[r], sb, preferred_element_type=F32)
                        + jnp.dot(a[r, r].astype(BF16), vnb, preferred_element_type=F32))
            state = (state * jnp.exp(glast[c * CHUNK:c * CHUNK + 1, :])
                     + lax.dot_general(kdec[r].astype(BF16), vnb, (((0,), (0,)), ((), ())),
                                       preferred_element_type=F32))
        s_ref[h] = state
        o = jnp.concatenate(outs, axis=0)
        o = o * lax.rsqrt(jnp.mean(o * o, axis=-1, keepdims=True) + RMS_EPS) * ng_ref[...]
        o_ref[:, h * GDN_DV:(h + 1) * GDN_DV] = (
            o * _silu(z_ref[:, h * GDN_DV:(h + 1) * GDN_DV])).astype(o_ref.dtype)

    ext_ref[0:halo, :] = ext_ref[tl:tl + halo, :]


def _gdn(qkv, z, ab, arow, cw, alog_r, dtb_r, alog_c, dtb_c, ng, *, batch, tl=256):
    t, nq = qkv.shape
    nl = t // batch // tl
    vw = z.shape[1]
    row = lambda b, l: (b * nl + l, 0)
    return pl.pallas_call(
        functools.partial(_gdn_kernel, tl=tl),
        grid=(batch, nl),
        in_specs=[pl.BlockSpec((tl, nq), row), pl.BlockSpec((tl, vw), row),
                  pl.BlockSpec((tl, LANES), row),
                  pl.BlockSpec((SUBLANES, tl), lambda b, l: (0, b * nl + l)),
                  _const_spec(cw.shape), _const_spec(alog_r.shape), _const_spec(dtb_r.shape),
                  _const_spec(alog_c.shape), _const_spec(dtb_c.shape), _const_spec(ng.shape)],
        out_specs=pl.BlockSpec((tl, vw), row),
        out_shape=jax.ShapeDtypeStruct((t, vw), BF16),
        scratch_shapes=[pltpu.VMEM((SUBLANES + tl, nq), F32),
                        pltpu.VMEM((GDN_HEADS, GDN_DK, GDN_DV), F32)],
        compiler_params=_params(2),
        name="gdn",
    )(qkv, z, ab, arow, cw, alog_r, dtb_r, alog_c, dtb_c, ng)


def _conv_module_kernel(c_ref, w_ref, b_ref, g_ref, beta_ref, o_ref, ext_ref, *, tl):
    halo = 4 * SUBLANES

    @pl.when(pl.program_id(1) == 0)
    def _():
        ext_ref[0:halo, :] = jnp.zeros((halo, ext_ref.shape[1]), F32)

    ext_ref[halo:halo + tl, :] = c_ref[...]
    acc = None
    for j in range(CONV_WIDTH):
        r0 = halo - (CONV_WIDTH - 1) + j
        term = w_ref[j:j + 1, :] * ext_ref[r0:r0 + tl, :]
        acc = term if acc is None else acc + term
    y = _layer_norm(acc + b_ref[...], g_ref[...], beta_ref[...])
    o_ref[...] = _silu(y).astype(o_ref.dtype)
    ext_ref[0:halo, :] = ext_ref[tl:tl + halo, :]


def _conv_module(c, w, b, g, beta, *, batch, tl=256):
    t, ch = c.shape
    nl = t // batch // tl
    row = lambda bi, l: (bi * nl + l, 0)
    return pl.pallas_call(
        functools.partial(_conv_module_kernel, tl=tl),
        grid=(batch, nl),
        in_specs=[pl.BlockSpec((tl, ch), row), _const_spec(w.shape), _const_spec(b.shape),
                  _const_spec(g.shape), _const_spec(beta.shape)],
        out_specs=pl.BlockSpec((tl, ch), row),
        out_shape=jax.ShapeDtypeStruct((t, ch), BF16),
        scratch_shapes=[pltpu.VMEM((4 * SUBLANES + tl, ch), F32)],
        compiler_params=_params(2),
        name="conv_module",
    )(c, w, b, g, beta)


def _merge_kernel(x_ref, o_ref, c_ref, sg_ref, wa_ref, wc_ref, bc_ref, wm_ref, g_ref, b_ref,
                  out_ref, *, alpha):
    d = x_ref.shape[1]
    y_a = jnp.dot(o_ref[...], wa_ref[...], preferred_element_type=F32)
    y_c = jnp.dot(c_ref[...], wc_ref[...], preferred_element_type=F32) + bc_ref[...]
    y = sg_ref[:, :d] * y_a + sg_ref[:, d:] * y_c
    mixed = jnp.dot(y.astype(BF16), wm_ref[...], preferred_element_type=F32)
    out_ref[...] = _layer_norm(alpha * x_ref[...] + mixed, g_ref[...], b_ref[...])


def _merge(x, o, c, sg, wa, wc, bc, wm, g, b, *, alpha, tm=512):
    t, d = x.shape
    row = lambda i: (i, 0)
    return pl.pallas_call(
        functools.partial(_merge_kernel, alpha=alpha),
        grid=(t // tm,),
        in_specs=[pl.BlockSpec((tm, d), row), pl.BlockSpec((tm, o.shape[1]), row),
                  pl.BlockSpec((tm, c.shape[1]), row), pl.BlockSpec((tm, sg.shape[1]), row),
                  _const_spec(wa.shape), _const_spec(wc.shape), _const_spec(bc.shape),
                  _const_spec(wm.shape), _const_spec(g.shape), _const_spec(b.shape)],
        out_specs=pl.BlockSpec((tm, d), row),
        out_shape=jax.ShapeDtypeStruct((t, d), F32),
        compiler_params=_params(1),
        name="merge",
    )(x, o, c, sg, wa, wc, bc, wm, g, b)


def _kv_proj_kernel(m_ref, w_ref, o_ref):
    o_ref[...] = jnp.dot(m_ref[...].astype(BF16), w_ref[...],
                         preferred_element_type=F32).astype(o_ref.dtype)


def _kv_proj(mem, w, *, tm=256):
    t, d = mem.shape
    n = w.shape[1]
    return pl.pallas_call(
        _kv_proj_kernel,
        grid=(t // tm,),
        in_specs=[pl.BlockSpec((tm, d), lambda i: (i, 0)), _const_spec(w.shape)],
        out_specs=pl.BlockSpec((tm, n), lambda i: (i, 0)),
        out_shape=jax.ShapeDtypeStruct((t, n), BF16),
        compiler_params=_params(1),
        name="kv_proj",
    )(mem, w)


def _xattn_kernel(x_ref, kv_ref, wq_ref, wo_ref, g_ref, b_ref, out_ref, *, alpha):
    x = x_ref[...]
    d = x.shape[1]
    hd = d // XATTN_HEADS
    q = jnp.dot(x.astype(BF16), wq_ref[...], preferred_element_type=F32).astype(BF16)
    heads = []
    for h in range(XATTN_HEADS):
        k = kv_ref[:, h * hd:(h + 1) * hd]
        v = kv_ref[:, d + h * hd:d + (h + 1) * hd]
        s = _mm_nt(q[:, h * hd:(h + 1) * hd], k) * (hd ** -0.5)
        e = jnp.exp(s - jnp.max(s, axis=-1, keepdims=True))
        p = e / jnp.sum(e, axis=-1, keepdims=True)
        heads.append(jnp.dot(p.astype(BF16), v, preferred_element_type=F32).astype(BF16))
    o = jnp.concatenate(heads, axis=1)
    att = jnp.dot(o, wo_ref[...], preferred_element_type=F32)
    out_ref[...] = _layer_norm(alpha * x + att, g_ref[...], b_ref[...])


def _xattn(x, kv, wq, wo, g, b, *, alpha, batch, tl=512):
    t, d = x.shape
    nl = t // batch // tl
    n_mem = kv.shape[0] // batch
    row = lambda bi, l: (bi * nl + l, 0)
    return pl.pallas_call(
        functools.partial(_xattn_kernel, alpha=alpha),
        grid=(batch, nl),
        in_specs=[pl.BlockSpec((tl, d), row),
                  pl.BlockSpec((n_mem, kv.shape[1]), lambda bi, l: (bi, 0)),
                  _const_spec(wq.shape), _const_spec(wo.shape),
                  _const_spec(g.shape), _const_spec(b.shape)],
        out_specs=pl.BlockSpec((tl, d), row),
        out_shape=jax.ShapeDtypeStruct((t, d), F32),
        compiler_params=_params(2),
        name="xattn",
    )(x, kv, wq, wo, g, b)


def _row(v):
    return v.reshape(1, -1).astype(F32)


def _pad_lanes(v, width=LANES):
    return jnp.pad(v, ((0, 0), (0, width - v.shape[1])))


def _layer(x, mem, p, *, batch, alpha):
    d = x.shape[1]
    kw = GDN_HEADS * GDN_DK
    vw = GDN_HEADS * GDN_DV
    qkv_w = 2 * kw + vw
    x = _ffn_ln(x, p["ffn1_wg"].astype(BF16), p["ffn1_wu"].astype(BF16), p["ffn1_wd"].astype(BF16),
                _row(p["ln1_g"]), _row(p["ln1_b"]), alpha=alpha)

    w_in = p["w_in"]
    o0 = qkv_w
    o1 = o0 + vw
    o2 = o1 + 2 * GDN_HEADS
    o3 = o2 + 2 * d
    w_ab = w_in[:, o1:o2]
    wab = jnp.pad(w_ab, ((0, 0), (0, LANES - 2 * GDN_HEADS))).astype(BF16)
    wabt = w_ab[:, :GDN_HEADS].T.astype(BF16)
    qkv, z, ab, arow, c, sg = _in_proj(
        x, w_in[:, :o0].astype(BF16), w_in[:, o0:o1].astype(BF16), wab, wabt,
        w_in[:, o2:o3].astype(BF16), w_in[:, o3:].astype(BF16))

    alog = _row(p["gdn_a_log"])
    dtb = _row(p["gdn_dt_bias"])
    o = _gdn(qkv, z, ab, arow, p["gdn_conv_qkv"].astype(F32),
             _pad_lanes(alog), _pad_lanes(dtb), alog.reshape(-1, 1), dtb.reshape(-1, 1),
             _row(p["gdn_norm_g"]), batch=batch)
    c = _conv_module(c, p["conv_dw_w"].astype(F32), _row(p["conv_dw_b"]), _row(p["conv_ln_g"]),
                     _row(p["conv_ln_b"]), batch=batch)
    x = _merge(x, o, c, sg, p["w_gdn_out"].astype(BF16), p["w_conv_out"].astype(BF16),
               _row(p["b_conv_out"]), p["w_mix_out"].astype(BF16),
               _row(p["ln2_g"]), _row(p["ln2_b"]), alpha=alpha)

    kv = _kv_proj(mem, p["w_xkv"].astype(BF16))
    x = _xattn(x, kv, p["w_xq"].astype(BF16), p["w_xo"].astype(BF16),
               _row(p["ln3_g"]), _row(p["ln3_b"]), alpha=alpha, batch=batch)

    return _ffn_ln(x, p["ffn2_wg"].astype(BF16), p["ffn2_wu"].astype(BF16), p["ffn2_wd"].astype(BF16),
                   _row(p["ln4_g"]), _row(p["ln4_b"]), alpha=alpha)


def kernel(x, mem, ffn1_wg, ffn1_wu, ffn1_wd, ln1_g, ln1_b, w_in, gdn_conv_qkv, gdn_a_log, gdn_dt_bias, gdn_norm_g, w_gdn_out, conv_dw_w, conv_dw_b, conv_ln_g, conv_ln_b, w_conv_out, b_conv_out, w_mix_out, ln2_g, ln2_b, w_xq, w_xkv, w_xo, ln3_g, ln3_b, ffn2_wg, ffn2_wu, ffn2_wd, ln4_g, ln4_b):
    weights = dict(
        ffn1_wg=ffn1_wg, ffn1_wu=ffn1_wu, ffn1_wd=ffn1_wd, ln1_g=ln1_g, ln1_b=ln1_b, w_in=w_in,
        gdn_conv_qkv=gdn_conv_qkv, gdn_a_log=gdn_a_log, gdn_dt_bias=gdn_dt_bias,
        gdn_norm_g=gdn_norm_g, w_gdn_out=w_gdn_out, conv_dw_w=conv_dw_w, conv_dw_b=conv_dw_b,
        conv_ln_g=conv_ln_g, conv_ln_b=conv_ln_b, w_conv_out=w_conv_out, b_conv_out=b_conv_out,
        w_mix_out=w_mix_out, ln2_g=ln2_g, ln2_b=ln2_b, w_xq=w_xq, w_xkv=w_xkv, w_xo=w_xo,
        ln3_g=ln3_g, ln3_b=ln3_b, ffn2_wg=ffn2_wg, ffn2_wu=ffn2_wu, ffn2_wd=ffn2_wd,
        ln4_g=ln4_g, ln4_b=ln4_b)
    batch, seq, d = x.shape
    depth = ffn1_wg.shape[0]
    alpha = (2 * depth) ** 0.25
    h = x.reshape(batch * seq, d)
    m = mem.reshape(-1, d)
    for i in range(depth):
        h = _layer(h, m, {k: v[i] for k, v in weights.items()}, batch=batch, alpha=alpha)
    return h.reshape(batch, seq, d)
```

```python
import functools

import jax
import jax.numpy as jnp
from jax import lax
from jax.experimental import pallas as pl
from jax.experimental.pallas import tpu as pltpu

F32 = jnp.float32
BF16 = jnp.bfloat16

LN_EPS = 1e-5
RMS_EPS = 1e-6
GDN_HEADS = 8
GDN_DK = 128
GDN_DV = 128
SHORT_CONV = 4
CHUNK = 64
CONV_WIDTH = 31
XATTN_HEADS = 4
N_MEM = 256

LANES = 128
SUBLANES = 8
BF16_ROWS = 16
VMEM_LIMIT = 56 * 1024 * 1024


def _params(n_axes):
    return pltpu.CompilerParams(dimension_semantics=("arbitrary",) * n_axes,
                                vmem_limit_bytes=VMEM_LIMIT)


def _const_spec(shape):
    zeros = (0,) * len(shape)
    return pl.BlockSpec(shape, lambda *_: zeros, pipeline_mode=pl.Buffered(1))


def _mm(a, b):
    return jnp.dot(a.astype(BF16), b.astype(BF16), preferred_element_type=F32)


def _mm_nt(a, b):
    return lax.dot_general(a.astype(BF16), b.astype(BF16), (((1,), (1,)), ((), ())),
                           preferred_element_type=F32)


def _split3(v):
    h1 = v.astype(BF16)
    r1 = v - h1.astype(F32)
    h2 = r1.astype(BF16)
    h3 = (r1 - h2.astype(F32)).astype(BF16)
    return h1, h2, h3


def _layer_norm(v, g, b):
    mu = jnp.mean(v, axis=-1, keepdims=True)
    d = v - mu
    var = jnp.mean(d * d, axis=-1, keepdims=True)
    return d * lax.rsqrt(var + LN_EPS) * g + b


def _silu(v):
    return v * jax.nn.sigmoid(v)


def _softplus(v):
    return jnp.maximum(v, 0.0) + jnp.log1p(jnp.exp(-jnp.abs(v)))


def _ffn_ln_kernel(x_ref, wg_ref, wu_ref, wd_ref, g_ref, b_ref, o_ref, *, alpha, fc):
    x = x_ref[...]
    xb = x.astype(BF16)
    acc = jnp.zeros(x.shape, F32)
    for j in range(wg_ref.shape[1] // fc):
        sl = slice(j * fc, (j + 1) * fc)
        gate = jnp.dot(xb, wg_ref[:, sl], preferred_element_type=F32)
        up = jnp.dot(xb, wu_ref[:, sl], preferred_element_type=F32)
        h = (_silu(gate) * up).astype(BF16)
        acc = acc + jnp.dot(h, wd_ref[sl, :], preferred_element_type=F32)
    o_ref[...] = _layer_norm(alpha * x + 0.5 * acc, g_ref[...], b_ref[...])


def _ffn_ln(x, wg, wu, wd, g, b, *, alpha, tm=512, fc=256):
    t, d = x.shape
    f = wg.shape[1]
    return pl.pallas_call(
        functools.partial(_ffn_ln_kernel, alpha=alpha, fc=fc),
        grid=(t // tm,),
        in_specs=[pl.BlockSpec((tm, d), lambda i: (i, 0)),
                  _const_spec((d, f)), _const_spec((d, f)), _const_spec((f, d)),
                  _const_spec((1, d)), _const_spec((1, d))],
        out_specs=pl.BlockSpec((tm, d), lambda i: (i, 0)),
        out_shape=jax.ShapeDtypeStruct((t, d), F32),
        compiler_params=_params(1),
        name="ffn_ln",
    )(x, wg, wu, wd, g, b)


def _in_proj_kernel(x_ref, wqkv_ref, wz_ref, wab_ref, wabt_ref, wglu_ref, wgate_ref,
                    qkv_ref, z_ref, ab_ref, arow_ref, c_ref, sg_ref):
    xb = x_ref[...].astype(BF16)
    qkv_ref[...] = jnp.dot(xb, wqkv_ref[...], preferred_element_type=F32)
    z_ref[...] = jnp.dot(xb, wz_ref[...], preferred_element_type=F32)
    ab_ref[...] = jnp.dot(xb, wab_ref[...], preferred_element_type=F32)
    arow_ref[...] = lax.dot_general(wabt_ref[...], xb, (((1,), (1,)), ((), ())),
                                    preferred_element_type=F32)
    glu = jnp.dot(xb, wglu_ref[...], preferred_element_type=F32)
    cc = glu.shape[1] // 2
    c_ref[...] = glu[:, :cc] * jax.nn.sigmoid(glu[:, cc:])
    sg_ref[...] = jax.nn.sigmoid(jnp.dot(xb, wgate_ref[...], preferred_element_type=F32))


def _in_proj(x, wqkv, wz, wab, wabt, wglu, wgate, *, tm=256):
    t, d = x.shape
    nq, nz, ng, ns = wqkv.shape[1], wz.shape[1], wglu.shape[1], wgate.shape[1]
    nr = wabt.shape[0]
    row = lambda i: (i, 0)
    return pl.pallas_call(
        _in_proj_kernel,
        grid=(t // tm,),
        in_specs=[pl.BlockSpec((tm, d), row),
                  _const_spec(wqkv.shape), _const_spec(wz.shape), _const_spec(wab.shape),
                  _const_spec(wabt.shape), _const_spec(wglu.shape), _const_spec(wgate.shape)],
        out_specs=[pl.BlockSpec((tm, nq), row), pl.BlockSpec((tm, nz), row),
                   pl.BlockSpec((tm, LANES), row), pl.BlockSpec((nr, tm), lambda i: (0, i)),
                   pl.BlockSpec((tm, ng // 2), row), pl.BlockSpec((tm, ns), row)],
        out_shape=[jax.ShapeDtypeStruct((t, nq), F32), jax.ShapeDtypeStruct((t, nz), F32),
                   jax.ShapeDtypeStruct((t, LANES), F32), jax.ShapeDtypeStruct((nr, t), F32),
                   jax.ShapeDtypeStruct((t, ng // 2), F32), jax.ShapeDtypeStruct((t, ns), F32)],
        compiler_params=_params(1),
        name="in_proj",
    )(x, wqkv, wz, wab, wabt, wglu, wgate)


def _gdn_kernel(qkv_ref, z_ref, ab_ref, arow_ref, cw_ref, alog_r_ref, dtb_r_ref, alog_c_ref,
                dtb_c_ref, ng_ref, o_ref, ext_ref, s_ref, *, tl):
    halo = SUBLANES
    kw = GDN_HEADS * GDN_DK
    nc = tl // CHUNK
    heads = range(GDN_HEADS)

    @pl.when(pl.program_id(1) == 0)
    def _():
        ext_ref[0:halo, :] = jnp.zeros((halo, ext_ref.shape[1]), F32)
        s_ref[...] = jnp.zeros(s_ref.shape, F32)

    ext_ref[halo:halo + tl, :] = qkv_ref[...]

    def conv_silu(c0):
        acc = None
        for j in range(SHORT_CONV):
            r0 = halo - (SHORT_CONV - 1) + j
            term = cw_ref[j:j + 1, c0:c0 + LANES] * ext_ref[r0:r0 + tl, c0:c0 + LANES]
            acc = term if acc is None else acc + term
        return _silu(acc)

    row = lax.broadcasted_iota(jnp.int32, (tl, tl), 0)
    col = lax.broadcasted_iota(jnp.int32, (tl, tl), 1)
    same = (row // CHUNK) == (col // CHUNK)
    same_b = jnp.where(same, 1.0, 0.0).astype(BF16)
    pi = lax.broadcasted_iota(jnp.int32, (CHUNK, tl), 0)
    lane = lax.broadcasted_iota(jnp.int32, (CHUNK, tl), 1)
    lc = lane // CHUNK
    lj = lane % CHUNK
    tri_p = pi >= lj
    strict_p = pi > lj
    eye_p = jnp.where(pi == lj, 1.0, 0.0)
    base = 16
    base_p = (pi // base) == (lj // base)

    def pack(full):
        p = full[0:CHUNK]
        for c in range(1, nc):
            p = jnp.where(lc == c, full[c * CHUNK:(c + 1) * CHUNK], p)
        return p

    def blockdiag(p):
        return jnp.concatenate([p.astype(BF16)] * nc, axis=0) * same_b

    def pmm(a, b):
        return jnp.dot(a.astype(BF16), blockdiag(b), preferred_element_type=F32)

    ab = ab_ref[...]
    g_col = -jnp.exp(alog_r_ref[...]) * _softplus(ab + dtb_r_ref[...])
    beta_all = jax.nn.sigmoid(ab)
    tri_b = jnp.where(same & (row >= col), 1.0, 0.0).astype(BF16)
    utri_b = jnp.where(same & (row <= col), 1.0, 0.0).astype(BF16)
    gc_col = sum(jnp.dot(tri_b, part, preferred_element_type=F32) for part in _split3(g_col))
    gl_col = jnp.concatenate(
        [jnp.broadcast_to(gc_col[(c + 1) * CHUNK - 1:(c + 1) * CHUNK, :], (CHUNK, LANES))
         for c in range(nc)], axis=0)
    g_row = -jnp.exp(alog_c_ref[...]) * _softplus(arow_ref[...] + dtb_c_ref[...])
    gc_row = sum(jnp.dot(part, utri_b, preferred_element_type=F32) for part in _split3(g_row))

    m, a, rhs, qg, kdec, glast = [], [], [], [], [], []
    for h in heads:
        q = conv_silu(h * GDN_DK)
        k = conv_silu(kw + h * GDN_DK)
        v = conv_silu(2 * kw + h * GDN_DV)
        q = q * lax.rsqrt(jnp.sum(q * q, axis=-1, keepdims=True) + RMS_EPS) * (GDN_DK ** -0.5)
        k = k * lax.rsqrt(jnp.sum(k * k, axis=-1, keepdims=True) + RMS_EPS)
        beta = beta_all[:, GDN_HEADS + h:GDN_HEADS + h + 1]
        gcol = gc_col[:, h:h + 1]
        gl = gl_col[:, h:h + 1]
        kb = k * beta
        e_g = jnp.exp(gcol)
        g_pack = pack(jnp.broadcast_to(gcol, (tl, tl)))
        diff = g_pack - gc_row[h:h + 1, :]
        decay = jnp.where(tri_p, jnp.exp(jnp.where(tri_p, diff, 0.0)), 0.0)
        m.append(jnp.where(strict_p, pack(_mm_nt(kb, k)) * decay, 0.0))
        a.append(pack(_mm_nt(q, k)) * decay)
        rhs.append(jnp.concatenate([v * beta, kb * e_g], axis=1).astype(BF16))
        qg.append((q * e_g).astype(BF16))
        kdec.append((k * jnp.exp(gl - gcol)).astype(BF16))
        glast.append(gl)

    n = [jnp.where(base_p, -m[h], 0.0) for h in heads]
    inv = [eye_p + n[h] for h in heads]
    for _ in range(3):
        n = [pmm(n[h], n[h]) for h in heads]
        inv = [inv[h] + pmm(n[h], inv[h]) for h in heads]
    s = base
    while s < CHUNK:
        lower_left = ((pi // (2 * s)) == (lj // (2 * s))) & ((pi // s) == (lj // s) + 1)
        e = [pmm(jnp.where(lower_left, m[h], 0.0), inv[h]) for h in heads]
        inv = [inv[h] - pmm(inv[h], e[h]) for h in heads]
        s *= 2
    x = [jnp.dot(blockdiag(inv[h]), rhs[h], preferred_element_type=F32) for h in heads]

    state = [s_ref[h] for h in heads]
    outs = [[] for _ in heads]
    for c in range(nc):
        r = slice(c * CHUNK, (c + 1) * CHUNK)
        for h in heads:
            sb = state[h].astype(BF16)
            v_new = x[h][r, :GDN_DV] - jnp.dot(x[h][r, GDN_DV:].astype(BF16), sb,
                                               preferred_element_type=F32)
            vnb = v_new.astype(BF16)
            outs[h].append(jnp.dot(qg[h][r], sb, preferred_element_type=F32)
                           + jnp.dot(a[h][:, r].astype(BF16), vnb, preferred_element_type=F32))
            state[h] = (state[h] * jnp.exp(glast[h][c * CHUNK:c * CHUNK + 1, :])
                        + lax.dot_general(kdec[h][r], vnb, (((0,), (0,)), ((), ())),
                                          preferred_element_type=F32))

    for h in heads:
        s_ref[h] = state[h]
        o = jnp.concatenate(outs[h], axis=0)
        o = o * lax.rsqrt(jnp.mean(o * o, axis=-1, keepdims=True) + RMS_EPS) * ng_ref[...]
        o_ref[:, h * GDN_DV:(h + 1) * GDN_DV] = (
            o * _silu(z_ref[:, h * GDN_DV:(h + 1) * GDN_DV])).astype(o_ref.dtype)

    ext_ref[0:halo, :] = ext_ref[tl:tl + halo, :]


def _gdn(qkv, z, ab, arow, cw, alog_r, dtb_r, alog_c, dtb_c, ng, *, batch, tl=256):
    t, nq = qkv.shape
    nl = t // batch // tl
    vw = z.shape[1]
    nr = arow.shape[0]
    row = lambda b, l: (b * nl + l, 0)
    return pl.pallas_call(
        functools.partial(_gdn_kernel, tl=tl),
        grid=(batch, nl),
        in_specs=[pl.BlockSpec((tl, nq), row), pl.BlockSpec((tl, vw), row),
                  pl.BlockSpec((tl, LANES), row),
                  pl.BlockSpec((nr, tl), lambda b, l: (0, b * nl + l)),
                  _const_spec(cw.shape), _const_spec(alog_r.shape), _const_spec(dtb_r.shape),
                  _const_spec(alog_c.shape), _const_spec(dtb_c.shape), _const_spec(ng.shape)],
        out_specs=pl.BlockSpec((tl, vw), row),
        out_shape=jax.ShapeDtypeStruct((t, vw), BF16),
        scratch_shapes=[pltpu.VMEM((SUBLANES + tl, nq), F32),
                        pltpu.VMEM((GDN_HEADS, GDN_DK, GDN_DV), F32)],
        compiler_params=_params(2),
        name="gdn",
    )(qkv, z, ab, arow, cw, alog_r, dtb_r, alog_c, dtb_c, ng)


def _conv_module_kernel(c_ref, w_ref, b_ref, g_ref, beta_ref, o_ref, ext_ref, y_ref, *, tl, rb):
    halo = 5 * SUBLANES
    ch = c_ref.shape[1]

    @pl.when(pl.program_id(1) == 0)
    def _():
        ext_ref[0:halo, :] = jnp.zeros((halo, ch), F32)

    ext_ref[halo:halo + tl, :] = c_ref[...]
    for c0 in range(0, ch, LANES):
        cols = slice(c0, c0 + LANES)
        for t0 in range(0, tl, rb):
            acc = None
            for r in range(SUBLANES):
                z = None
                for q in range((CONV_WIDTH - 1 - r) // SUBLANES + 1):
                    j = CONV_WIDTH - 1 - (SUBLANES * q + r)
                    lo = halo + t0 - SUBLANES * (q + 1)
                    term = w_ref[j:j + 1, cols] * ext_ref[lo:lo + rb + SUBLANES, cols]
                    z = term if z is None else z + term
                sh = z[SUBLANES - r:SUBLANES - r + rb]
                acc = sh if acc is None else acc + sh
            y_ref[t0:t0 + rb, cols] = acc
    y = _layer_norm(y_ref[...] + b_ref[...], g_ref[...], beta_ref[...])
    o_ref[...] = _silu(y).astype(o_ref.dtype)
    ext_ref[0:halo, :] = ext_ref[tl:tl + halo, :]


def _conv_module(c, w, b, g, beta, *, batch, tl=256, rb=64):
    t, ch = c.shape
    nl = t // batch // tl
    row = lambda bi, l: (bi * nl + l, 0)
    return pl.pallas_call(
        functools.partial(_conv_module_kernel, tl=tl, rb=rb),
        grid=(batch, nl),
        in_specs=[pl.BlockSpec((tl, ch), row), _const_spec(w.shape), _const_spec(b.shape),
                  _const_spec(g.shape), _const_spec(beta.shape)],
        out_specs=pl.BlockSpec((tl, ch), row),
        out_shape=jax.ShapeDtypeStruct((t, ch), BF16),
        scratch_shapes=[pltpu.VMEM((5 * SUBLANES + tl, ch), F32), pltpu.VMEM((tl, ch), F32)],
        compiler_params=_params(2),
        name="conv_module",
    )(c, w, b, g, beta)


def _merge_kernel(x_ref, o_ref, c_ref, sg_ref, wa_ref, wc_ref, bc_ref, wm_ref, g_ref, b_ref,
                  out_ref, *, alpha):
    d = x_ref.shape[1]
    y_a = jnp.dot(o_ref[...], wa_ref[...], preferred_element_type=F32)
    y_c = jnp.dot(c_ref[...], wc_ref[...], preferred_element_type=F32) + bc_ref[...]
    y = sg_ref[:, :d] * y_a + sg_ref[:, d:] * y_c
    mixed = jnp.dot(y.astype(BF16), wm_ref[...], preferred_element_type=F32)
    out_ref[...] = _layer_norm(alpha * x_ref[...] + mixed, g_ref[...], b_ref[...])


def _merge(x, o, c, sg, wa, wc, bc, wm, g, b, *, alpha, tm=512):
    t, d = x.shape
    row = lambda i: (i, 0)
    return pl.pallas_call(
        functools.partial(_merge_kernel, alpha=alpha),
        grid=(t // tm,),
        in_specs=[pl.BlockSpec((tm, d), row), pl.BlockSpec((tm, o.shape[1]), row),
                  pl.BlockSpec((tm, c.shape[1]), row), pl.BlockSpec((tm, sg.shape[1]), row),
                  _const_spec(wa.shape), _const_spec(wc.shape), _const_spec(bc.shape),
                  _const_spec(wm.shape), _const_spec(g.shape), _const_spec(b.shape)],
        out_specs=pl.BlockSpec((tm, d), row),
        out_shape=jax.ShapeDtypeStruct((t, d), F32),
        compiler_params=_params(1),
        name="merge",
    )(x, o, c, sg, wa, wc, bc, wm, g, b)


def _kv_proj_kernel(m_ref, w_ref, o_ref):
    o_ref[...] = jnp.dot(m_ref[...].astype(BF16), w_ref[...],
                         preferred_element_type=F32).astype(o_ref.dtype)


def _kv_proj(mem, w, *, tm=256):
    t, d = mem.shape
    n = w.shape[1]
    return pl.pallas_call(
        _kv_proj_kernel,
        grid=(t // tm,),
        in_specs=[pl.BlockSpec((tm, d), lambda i: (i, 0)), _const_spec(w.shape)],
        out_specs=pl.BlockSpec((tm, n), lambda i: (i, 0)),
        out_shape=jax.ShapeDtypeStruct((t, n), BF16),
        compiler_params=_params(1),
        name="kv_proj",
    )(mem, w)


def _xattn_kernel(x_ref, kv_ref, wq_ref, wo_ref, g_ref, b_ref, out_ref, *, alpha):
    x = x_ref[...]
    d = x.shape[1]
    hd = d // XATTN_HEADS
    q = jnp.dot(x.astype(BF16), wq_ref[...], preferred_element_type=F32).astype(BF16)
    heads = []
    for h in range(XATTN_HEADS):
        k = kv_ref[:, h * hd:(h + 1) * hd]
        v = kv_ref[:, d + h * hd:d + (h + 1) * hd]
        s = _mm_nt(q[:, h * hd:(h + 1) * hd], k) * (hd ** -0.5)
        e = jnp.exp(s - jnp.max(s, axis=-1, keepdims=True))
        p = e / jnp.sum(e, axis=-1, keepdims=True)
        heads.append(jnp.dot(p.astype(BF16), v, preferred_element_type=F32).astype(BF16))
    o = jnp.concatenate(heads, axis=1)
    att = jnp.dot(o, wo_ref[...], preferred_element_type=F32)
    out_ref[...] = _layer_norm(alpha * x + att, g_ref[...], b_ref[...])


def _xattn(x, kv, wq, wo, g, b, *, alpha, batch, tl=512):
    t, d = x.shape
    nl = t // batch // tl
    n_mem = kv.shape[0] // batch
    row = lambda bi, l: (bi * nl + l, 0)
    return pl.pallas_call(
        functools.partial(_xattn_kernel, alpha=alpha),
        grid=(batch, nl),
        in_specs=[pl.BlockSpec((tl, d), row),
                  pl.BlockSpec((n_mem, kv.shape[1]), lambda bi, l: (bi, 0)),
                  _const_spec(wq.shape), _const_spec(wo.shape),
                  _const_spec(g.shape), _const_spec(b.shape)],
        out_specs=pl.BlockSpec((tl, d), row),
        out_shape=jax.ShapeDtypeStruct((t, d), F32),
        compiler_params=_params(2),
        name="xattn",
    )(x, kv, wq, wo, g, b)


def _row(v):
    return v.reshape(1, -1).astype(F32)


def _pad_to(v, shape):
    return jnp.pad(v, tuple((0, n - s) for s, n in zip(v.shape, shape)))


def _layer(x, mem, p, *, batch, alpha):
    d = x.shape[1]
    kw = GDN_HEADS * GDN_DK
    vw = GDN_HEADS * GDN_DV
    qkv_w = 2 * kw + vw
    x = _ffn_ln(x, p["ffn1_wg"].astype(BF16), p["ffn1_wu"].astype(BF16), p["ffn1_wd"].astype(BF16),
                _row(p["ln1_g"]), _row(p["ln1_b"]), alpha=alpha)

    w_in = p["w_in"]
    o0 = qkv_w
    o1 = o0 + vw
    o2 = o1 + 2 * GDN_HEADS
    o3 = o2 + 2 * d
    w_ab = w_in[:, o1:o2]
    wab = _pad_to(w_ab, (d, LANES)).astype(BF16)
    wabt = _pad_to(w_ab[:, :GDN_HEADS].T, (BF16_ROWS, d)).astype(BF16)
    qkv, z, ab, arow, c, sg = _in_proj(
        x, w_in[:, :o0].astype(BF16), w_in[:, o0:o1].astype(BF16), wab, wabt,
        w_in[:, o2:o3].astype(BF16), w_in[:, o3:].astype(BF16))

    alog = _row(p["gdn_a_log"])
    dtb = _row(p["gdn_dt_bias"])
    o = _gdn(qkv, z, ab, arow, p["gdn_conv_qkv"].astype(F32),
             _pad_to(alog, (1, LANES)), _pad_to(dtb, (1, LANES)),
             _pad_to(alog.reshape(-1, 1), (BF16_ROWS, 1)), _pad_to(dtb.reshape(-1, 1), (BF16_ROWS, 1)),
             _row(p["gdn_norm_g"]), batch=batch)
    c = _conv_module(c, p["conv_dw_w"].astype(F32), _row(p["conv_dw_b"]), _row(p["conv_ln_g"]),
                     _row(p["conv_ln_b"]), batch=batch)
    x = _merge(x, o, c, sg, p["w_gdn_out"].astype(BF16), p["w_conv_out"].astype(BF16),
               _row(p["b_conv_out"]), p["w_mix_out"].astype(BF16),
               _row(p["ln2_g"]), _row(p["ln2_b"]), alpha=alpha)

    kv = _kv_proj(mem, p["w_xkv"].astype(BF16))
    x = _xattn(x, kv, p["w_xq"].astype(BF16), p["w_xo"].astype(BF16),
               _row(p["ln3_g"]), _row(p["ln3_b"]), alpha=alpha, batch=batch)

    return _ffn_ln(x, p["ffn2_wg"].astype(BF16), p["ffn2_wu"].astype(BF16), p["ffn2_wd"].astype(BF16),
                   _row(p["ln4_g"]), _row(p["ln4_b"]), alpha=alpha)


def kernel(x, mem, ffn1_wg, ffn1_wu, ffn1_wd, ln1_g, ln1_b, w_in, gdn_conv_qkv, gdn_a_log, gdn_dt_bias, gdn_norm_g, w_gdn_out, conv_dw_w, conv_dw_b, conv_ln_g, conv_ln_b, w_conv_out, b_conv_out, w_mix_out, ln2_g, ln2_b, w_xq, w_xkv, w_xo, ln3_g, ln3_b, ffn2_wg, ffn2_wu, ffn2_wd, ln4_g, ln4_b):
    weights = dict(
        ffn1_wg=ffn1_wg, ffn1_wu=ffn1_wu, ffn1_wd=ffn1_wd, ln1_g=ln1_g, ln1_b=ln1_b, w_in=w_in,
        gdn_conv_qkv=gdn_conv_qkv, gdn_a_log=gdn_a_log, gdn_dt_bias=gdn_dt_bias,
        gdn_norm_g=gdn_norm_g, w_gdn_out=w_gdn_out, conv_dw_w=conv_dw_w, conv_dw_b=conv_dw_b,
        conv_ln_g=conv_ln_g, conv_ln_b=conv_ln_b, w_conv_out=w_conv_out, b_conv_out=b_conv_out,
        w_mix_out=w_mix_out, ln2_g=ln2_g, ln2_b=ln2_b, w_xq=w_xq, w_xkv=w_xkv, w_xo=w_xo,
        ln3_g=ln3_g, ln3_b=ln3_b, ffn2_wg=ffn2_wg, ffn2_wu=ffn2_wu, ffn2_wd=ffn2_wd,
        ln4_g=ln4_g, ln4_b=ln4_b)
    batch, seq, d = x.shape
    depth = ffn1_wg.shape[0]
    alpha = (2 * depth) ** 0.25
    h = x.reshape(batch * seq, d)
    m = mem.reshape(-1, d)
    for i in range(depth):
        h = _layer(h, m, {k: v[i] for k, v in weights.items()}, batch=batch, alpha=alpha)
    return h.reshape(batch, seq, d)
```

```python
import functools

import jax
import jax.numpy as jnp
from jax import lax
from jax.experimental import pallas as pl
from jax.experimental.pallas import tpu as pltpu

F32 = jnp.float32
BF16 = jnp.bfloat16

LN_EPS = 1e-5
RMS_EPS = 1e-6
GDN_HEADS = 8
GDN_DK = 128
GDN_DV = 128
SHORT_CONV = 4
CHUNK = 64
CONV_WIDTH = 31
XATTN_HEADS = 4
N_MEM = 256

LANES = 128
SUBLANES = 8
BF16_ROWS = 16
VMEM_LIMIT = 56 * 1024 * 1024
CONV_HALO = 5 * SUBLANES


def _params(n_axes):
    return pltpu.CompilerParams(dimension_semantics=("arbitrary",) * n_axes,
                                vmem_limit_bytes=VMEM_LIMIT)


def _const_spec(shape):
    zeros = (0,) * len(shape)
    return pl.BlockSpec(shape, lambda *_: zeros, pipeline_mode=pl.Buffered(1))


def _mm_nt(a, b):
    return lax.dot_general(a.astype(BF16), b.astype(BF16), (((1,), (1,)), ((), ())),
                           preferred_element_type=F32)


def _split3(v):
    h1 = v.astype(BF16)
    r1 = v - h1.astype(F32)
    h2 = r1.astype(BF16)
    h3 = (r1 - h2.astype(F32)).astype(BF16)
    return h1, h2, h3


def _layer_norm(v, g, b):
    mu = jnp.mean(v, axis=-1, keepdims=True)
    d = v - mu
    var = jnp.mean(d * d, axis=-1, keepdims=True)
    return d * lax.rsqrt(var + LN_EPS) * g + b


def _silu(v):
    return v * jax.nn.sigmoid(v)


def _softplus(v):
    return jnp.maximum(v, 0.0) + jnp.log1p(jnp.exp(-jnp.abs(v)))


def _ffn_residual(x, wg_ref, wu_ref, wd_ref, *, alpha, fc):
    xb = x.astype(BF16)
    acc = jnp.zeros(x.shape, F32)
    for j in range(wg_ref.shape[1] // fc):
        sl = slice(j * fc, (j + 1) * fc)
        gate = jnp.dot(xb, wg_ref[:, sl], preferred_element_type=F32)
        up = jnp.dot(xb, wu_ref[:, sl], preferred_element_type=F32)
        h = (_silu(gate) * up).astype(BF16)
        acc = acc + jnp.dot(h, wd_ref[sl, :], preferred_element_type=F32)
    return alpha * x + 0.5 * acc


def _ffn_ln_kernel(x_ref, wg_ref, wu_ref, wd_ref, g_ref, b_ref, o_ref, *, alpha, fc):
    y = _ffn_residual(x_ref[...], wg_ref, wu_ref, wd_ref, alpha=alpha, fc=fc)
    o_ref[...] = _layer_norm(y, g_ref[...], b_ref[...])


def _ffn_ln(x, wg, wu, wd, g, b, *, alpha, tm=512, fc=256):
    t, d = x.shape
    consts = (wg, wu, wd, g, b)
    return pl.pallas_call(
        functools.partial(_ffn_ln_kernel, alpha=alpha, fc=fc),
        grid=(t // tm,),
        in_specs=[pl.BlockSpec((tm, d), lambda i: (i, 0))] + [_const_spec(w.shape) for w in consts],
        out_specs=pl.BlockSpec((tm, d), lambda i: (i, 0)),
        out_shape=jax.ShapeDtypeStruct((t, d), F32),
        compiler_params=_params(1),
        name="ffn_ln",
    )(x, *consts)


def _ffn_prep_kernel(x_ref, wg_ref, wu_ref, wd_ref, lg_ref, lb_ref, wqkv_ref, wab_ref, wabt_ref, cw_ref,
                     alog_r_ref, dtb_r_ref, alog_c_ref, dtb_c_ref,
                     x1_ref, q_ref, k_ref, kb_ref, qg_ref, kd_ref, rhs_ref, gc_ref, gl_ref, gcr_ref,
                     ext_ref, *, alpha, fc, tl):
    halo = SUBLANES
    kw = GDN_HEADS * GDN_DK
    nc = tl // CHUNK

    @pl.when(pl.program_id(1) == 0)
    def _():
        ext_ref[0:halo, :] = jnp.zeros((halo, ext_ref.shape[1]), F32)

    y = _ffn_residual(x_ref[...], wg_ref, wu_ref, wd_ref, alpha=alpha, fc=fc)
    x1 = _layer_norm(y, lg_ref[...], lb_ref[...])
    x1_ref[...] = x1
    xb = x1.astype(BF16)
    ext_ref[halo:halo + tl, :] = jnp.dot(xb, wqkv_ref[...], preferred_element_type=F32)
    ab = jnp.dot(xb, wab_ref[...], preferred_element_type=F32)
    arow = lax.dot_general(wabt_ref[...], xb, (((1,), (1,)), ((), ())), preferred_element_type=F32)

    row = lax.broadcasted_iota(jnp.int32, (tl, tl), 0)
    col = lax.broadcasted_iota(jnp.int32, (tl, tl), 1)
    same = (row // CHUNK) == (col // CHUNK)
    tri_b = jnp.where(same & (row >= col), 1.0, 0.0).astype(BF16)
    utri_b = jnp.where(same & (row <= col), 1.0, 0.0).astype(BF16)
    g_col = -jnp.exp(alog_r_ref[...]) * _softplus(ab + dtb_r_ref[...])
    gc_col = sum(jnp.dot(tri_b, part, preferred_element_type=F32) for part in _split3(g_col))
    gl_col = jnp.concatenate(
        [jnp.broadcast_to(gc_col[(c + 1) * CHUNK - 1:(c + 1) * CHUNK, :], (CHUNK, LANES))
         for c in range(nc)], axis=0)
    g_row = -jnp.exp(alog_c_ref[...]) * _softplus(arow + dtb_c_ref[...])
    gcr_ref[...] = sum(jnp.dot(part, utri_b, preferred_element_type=F32) for part in _split3(g_row))
    gc_ref[...] = gc_col
    gl_ref[...] = gl_col
    beta_all = jax.nn.sigmoid(ab)
    e_all = jnp.exp(gc_col)
    d_all = jnp.exp(gl_col - gc_col)

    def conv_silu(c0):
        acc = None
        for j in reversed(range(SHORT_CONV)):
            r0 = halo - (SHORT_CONV - 1) + j
            term = cw_ref[j:j + 1, c0:c0 + LANES] * ext_ref[r0:r0 + tl, c0:c0 + LANES]
            acc = term if acc is None else acc + term
        return _silu(acc)

    for h in range(GDN_HEADS):
        hs = slice(h * GDN_DK, (h + 1) * GDN_DK)
        q = conv_silu(h * GDN_DK)
        k = conv_silu(kw + h * GDN_DK)
        v = conv_silu(2 * kw + h * GDN_DV)
        q = q * (lax.rsqrt(jnp.sum(q * q, axis=-1, keepdims=True) + RMS_EPS) * (GDN_DK ** -0.5))
        k = k * lax.rsqrt(jnp.sum(k * k, axis=-1, keepdims=True) + RMS_EPS)
        beta = beta_all[:, GDN_HEADS + h:GDN_HEADS + h + 1]
        e_g = e_all[:, h:h + 1]
        kb = k * beta
        q_ref[:, hs] = q.astype(BF16)
        k_ref[:, hs] = k.astype(BF16)
        kb_ref[:, hs] = kb.astype(BF16)
        qg_ref[:, hs] = (q * e_g).astype(BF16)
        kd_ref[:, hs] = (k * d_all[:, h:h + 1]).astype(BF16)
        rhs_ref[:, 2 * h * GDN_DV:(2 * h + 1) * GDN_DV] = (v * beta).astype(BF16)
        rhs_ref[:, (2 * h + 1) * GDN_DV:(2 * h + 2) * GDN_DV] = (kb * e_g).astype(BF16)

    ext_ref[0:halo, :] = ext_ref[tl:tl + halo, :]


def _ffn_prep(x, wg, wu, wd, lg, lb, wqkv, wab, wabt, cw, alog_r, dtb_r, alog_c, dtb_c,
              *, alpha, batch, tl=256, fc=256):
    t, d = x.shape
    nl = t // batch // tl
    kw = GDN_HEADS * GDN_DK
    vw = GDN_HEADS * GDN_DV
    nr = wabt.shape[0]
    row = lambda b, l: (b * nl + l, 0)
    consts = (wg, wu, wd, lg, lb, wqkv, wab, wabt, cw, alog_r, dtb_r, alog_c, dtb_c)
    wide = lambda n, dt: (pl.BlockSpec((tl, n), row), jax.ShapeDtypeStruct((t, n), dt))
    outs = [wide(d, F32)] + [wide(kw, BF16)] * 5 + [wide(2 * vw, BF16), wide(LANES, F32), wide(LANES, F32),
                                                    (pl.BlockSpec((nr, tl), lambda b, l: (0, b * nl + l)),
                                                     jax.ShapeDtypeStruct((nr, t), F32))]
    return pl.pallas_call(
        functools.partial(_ffn_prep_kernel, alpha=alpha, fc=fc, tl=tl),
        grid=(batch, nl),
        in_specs=[pl.BlockSpec((tl, d), row)] + [_const_spec(w.shape) for w in consts],
        out_specs=[o[0] for o in outs],
        out_shape=[o[1] for o in outs],
        scratch_shapes=[pltpu.VMEM((SUBLANES + tl, wqkv.shape[1]), F32)],
        compiler_params=_params(2),
        name="ffn_prep",
    )(x, *consts)


def _conv31_cols(w_ref, ext_ref, y_ref, c0, *, tl, rb):
    cols = slice(c0, c0 + LANES)
    for t0 in range(0, tl, rb):
        acc = None
        for r in range(SUBLANES):
            z = None
            for q in range((CONV_WIDTH - 1 - r) // SUBLANES + 1):
                j = CONV_WIDTH - 1 - (SUBLANES * q + r)
                lo = CONV_HALO + t0 - SUBLANES * (q + 1)
                term = w_ref[j:j + 1, cols] * ext_ref[lo:lo + rb + SUBLANES, cols]
                z = term if z is None else z + term
            sh = z[SUBLANES - r:SUBLANES - r + rb]
            acc = sh if acc is None else acc + sh
        y_ref[t0:t0 + rb, cols] = acc


def _mixer_kernel(x_ref, q_ref, k_ref, kb_ref, qg_ref, kd_ref, rhs_ref, gc_ref, gl_ref, gcr_ref,
                  ng_ref, wz_ref, wglu_ref, wgate_ref, cw_ref, cb_ref, lg_ref, lb_ref,
                  o_ref, cact_ref, sg_ref, s_ref, ext_ref, y_ref, *, tl, rb):
    nc = tl // CHUNK
    heads = range(GDN_HEADS)
    ch = ext_ref.shape[1]

    @pl.when(pl.program_id(1) == 0)
    def _():
        s_ref[...] = jnp.zeros(s_ref.shape, F32)
        ext_ref[0:CONV_HALO, :] = jnp.zeros((CONV_HALO, ch), F32)

    xb = x_ref[...].astype(BF16)
    glu = jnp.dot(xb, wglu_ref[...], preferred_element_type=F32)
    ext_ref[CONV_HALO:CONV_HALO + tl, :] = glu[:, :ch] * jax.nn.sigmoid(glu[:, ch:])
    for c0 in range(0, ch, LANES):
        _conv31_cols(cw_ref, ext_ref, y_ref, c0, tl=tl, rb=rb)
    cact_ref[...] = _silu(_layer_norm(y_ref[...] + cb_ref[...], lg_ref[...], lb_ref[...])
                          ).astype(cact_ref.dtype)
    ext_ref[0:CONV_HALO, :] = ext_ref[tl:tl + CONV_HALO, :]
    zs = _silu(jnp.dot(xb, wz_ref[...], preferred_element_type=F32))
    sg_ref[...] = jax.nn.sigmoid(
        jnp.dot(xb, wgate_ref[...], preferred_element_type=F32)).astype(sg_ref.dtype)

    row = lax.broadcasted_iota(jnp.int32, (tl, tl), 0)
    col = lax.broadcasted_iota(jnp.int32, (tl, tl), 1)
    same_b = jnp.where((row // CHUNK) == (col // CHUNK), 1.0, 0.0).astype(BF16)
    pi = lax.broadcasted_iota(jnp.int32, (CHUNK, tl), 0)
    lane = lax.broadcasted_iota(jnp.int32, (CHUNK, tl), 1)
    lc = lane // CHUNK
    lj = lane % CHUNK
    tri_p = pi >= lj
    strict_p = pi > lj
    eye_p = jnp.where(pi == lj, 1.0, 0.0)
    base = 16
    base_p = (pi // base) == (lj // base)

    def pack(full):
        p = full[0:CHUNK]
        for c in range(1, nc):
            p = jnp.where(lc == c, full[c * CHUNK:(c + 1) * CHUNK], p)
        return p

    def blockdiag(p):
        return jnp.concatenate([p.astype(BF16)] * nc, axis=0) * same_b

    def pmm(a, b):
        return jnp.dot(a.astype(BF16), blockdiag(b), preferred_element_type=F32)

    m, a = [], []
    for h in heads:
        hs = slice(h * GDN_DK, (h + 1) * GDN_DK)
        k = k_ref[:, hs]
        g_pack = pack(jnp.broadcast_to(gc_ref[:, h:h + 1], (tl, tl)))
        diff = g_pack - gcr_ref[h:h + 1, :]
        decay = jnp.where(tri_p, jnp.exp(jnp.where(tri_p, diff, 0.0)), 0.0)
        m.append(jnp.where(strict_p, pack(_mm_nt(kb_ref[:, hs], k)) * decay, 0.0))
        a.append(pack(_mm_nt(q_ref[:, hs], k)) * decay)

    n = [jnp.where(base_p, -m[h], 0.0) for h in heads]
    inv = [eye_p + n[h] for h in heads]
    for _ in range(3):
        n = [pmm(n[h], n[h]) for h in heads]
        inv = [inv[h] + pmm(n[h], inv[h]) for h in heads]
    s = base
    while s < CHUNK:
        lower_left = ((pi // (2 * s)) == (lj // (2 * s))) & ((pi // s) == (lj // s) + 1)
        e = [pmm(jnp.where(lower_left, m[h], 0.0), inv[h]) for h in heads]
        inv = [inv[h] - pmm(inv[h], e[h]) for h in heads]
        s *= 2
    xb16 = [jnp.dot(blockdiag(inv[h]), rhs_ref[:, 2 * h * GDN_DV:(2 * h + 2) * GDN_DV],
                    preferred_element_type=F32).astype(BF16) for h in heads]

    ax = [jnp.dot(blockdiag(a[h]), xb16[h], preferred_element_type=F32) for h in heads]
    kx = [[lax.dot_general(kd_ref[c * CHUNK:(c + 1) * CHUNK, h * GDN_DK:(h + 1) * GDN_DK],
                           xb16[h][c * CHUNK:(c + 1) * CHUNK], (((0,), (0,)), ((), ())),
                           preferred_element_type=F32) for c in range(nc)] for h in heads]
    state = [s_ref[h] for h in heads]
    outs = [[] for _ in heads]
    for c in range(nc):
        r = slice(c * CHUNK, (c + 1) * CHUNK)
        for h in heads:
            hs = slice(h * GDN_DK, (h + 1) * GDN_DK)
            sb = state[h].astype(BF16)
            r_c = (qg_ref[r, hs].astype(F32) - ax[h][r, GDN_DV:]).astype(BF16)
            outs[h].append(jnp.dot(r_c, sb, preferred_element_type=F32) + ax[h][r, :GDN_DV])
            state[h] = (state[h] * jnp.exp(gl_ref[c * CHUNK:c * CHUNK + 1, h:h + 1])
                        - jnp.dot(kx[h][c][:, GDN_DV:].astype(BF16), sb, preferred_element_type=F32)
                        + kx[h][c][:, :GDN_DV])

    for h in heads:
        hs = slice(h * GDN_DV, (h + 1) * GDN_DV)
        s_ref[h] = state[h]
        o = jnp.concatenate(outs[h], axis=0)
        o = o * lax.rsqrt(jnp.mean(o * o, axis=-1, keepdims=True) + RMS_EPS) * ng_ref[...]
        o_ref[:, hs] = (o * zs[:, hs]).astype(o_ref.dtype)


def _mixer(x, q, k, kb, qg, kd, rhs, gc, gl, gcr, ng, wz, wglu, wgate, cw, cb, lg, lb,
           *, batch, tl=256, rb=64):
    t = x.shape[0]
    nl = t // batch // tl
    vw, ch, ns = wz.shape[1], wglu.shape[1] // 2, wgate.shape[1]
    row = lambda b, l: (b * nl + l, 0)
    tiles = (x, q, k, kb, qg, kd, rhs, gc, gl)
    consts = (ng, wz, wglu, wgate, cw, cb, lg, lb)
    return pl.pallas_call(
        functools.partial(_mixer_kernel, tl=tl, rb=rb),
        grid=(batch, nl),
        in_specs=([pl.BlockSpec((tl, v.shape[1]), row) for v in tiles]
                  + [pl.BlockSpec((gcr.shape[0], tl), lambda b, l: (0, b * nl + l))]
                  + [_const_spec(w.shape) for w in consts]),
        out_specs=[pl.BlockSpec((tl, vw), row), pl.BlockSpec((tl, ch), row), pl.BlockSpec((tl, ns), row)],
        out_shape=[jax.ShapeDtypeStruct((t, vw), BF16), jax.ShapeDtypeStruct((t, ch), BF16),
                   jax.ShapeDtypeStruct((t, ns), BF16)],
        scratch_shapes=[pltpu.VMEM((GDN_HEADS, GDN_DK, GDN_DV), F32),
                        pltpu.VMEM((CONV_HALO + tl, ch), F32), pltpu.VMEM((tl, ch), F32)],
        compiler_params=_params(2),
        name="mixer",
    )(*tiles, gcr, *consts)


def _merge_kernel(x_ref, o_ref, c_ref, sg_ref, wa_ref, wc_ref, bc_ref, wm_ref, g_ref, b_ref,
                  out_ref, *, alpha):
    d = x_ref.shape[1]
    y_a = jnp.dot(o_ref[...], wa_ref[...], preferred_element_type=F32)
    y_c = jnp.dot(c_ref[...], wc_ref[...], preferred_element_type=F32) + bc_ref[...]
    y = sg_ref[:, :d] * y_a + sg_ref[:, d:] * y_c
    mixed = jnp.dot(y.astype(BF16), wm_ref[...], preferred_element_type=F32)
    out_ref[...] = _layer_norm(alpha * x_ref[...] + mixed, g_ref[...], b_ref[...])


def _merge(x, o, c, sg, wa, wc, bc, wm, g, b, *, alpha, tm=512):
    t, d = x.shape
    row = lambda i: (i, 0)
    return pl.pallas_call(
        functools.partial(_merge_kernel, alpha=alpha),
        grid=(t // tm,),
        in_specs=[pl.BlockSpec((tm, d), row), pl.BlockSpec((tm, o.shape[1]), row),
                  pl.BlockSpec((tm, c.shape[1]), row), pl.BlockSpec((tm, sg.shape[1]), row),
                  _const_spec(wa.shape), _const_spec(wc.shape), _const_spec(bc.shape),
                  _const_spec(wm.shape), _const_spec(g.shape), _const_spec(b.shape)],
        out_specs=pl.BlockSpec((tm, d), row),
        out_shape=jax.ShapeDtypeStruct((t, d), F32),
        compiler_params=_params(1),
        name="merge",
    )(x, o, c, sg, wa, wc, bc, wm, g, b)


def _kv_proj_kernel(m_ref, w_ref, o_ref):
    o_ref[...] = jnp.dot(m_ref[...].astype(BF16), w_ref[...],
                         preferred_element_type=F32).astype(o_ref.dtype)


def _kv_proj(mem, w, *, tm=256):
    t, d = mem.shape
    n = w.shape[1]
    return pl.pallas_call(
        _kv_proj_kernel,
        grid=(t // tm,),
        in_specs=[pl.BlockSpec((tm, d), lambda i: (i, 0)), _const_spec(w.shape)],
        out_specs=pl.BlockSpec((tm, n), lambda i: (i, 0)),
        out_shape=jax.ShapeDtypeStruct((t, n), BF16),
        compiler_params=_params(1),
        name="kv_proj",
    )(mem, w)


def _xattn_kernel(x_ref, kv_ref, wq_ref, wo_ref, g_ref, b_ref, out_ref, *, alpha):
    x = x_ref[...]
    d = x.shape[1]
    hd = d // XATTN_HEADS
    q = jnp.dot(x.astype(BF16), wq_ref[...], preferred_element_type=F32).astype(BF16)
    heads = []
    for h in range(XATTN_HEADS):
        k = kv_ref[:, h * hd:(h + 1) * hd]
        v = kv_ref[:, d + h * hd:d + (h + 1) * hd]
        s = _mm_nt(q[:, h * hd:(h + 1) * hd], k) * (hd ** -0.5)
        e = jnp.exp(s - jnp.max(s, axis=-1, keepdims=True))
        p = e / jnp.sum(e, axis=-1, keepdims=True)
        heads.append(jnp.dot(p.astype(BF16), v, preferred_element_type=F32).astype(BF16))
    o = jnp.concatenate(heads, axis=1)
    att = jnp.dot(o, wo_ref[...], preferred_element_type=F32)
    out_ref[...] = _layer_norm(alpha * x + att, g_ref[...], b_ref[...])


def _xattn(x, kv, wq, wo, g, b, *, alpha, batch, tl=512):
    t, d = x.shape
    nl = t // batch // tl
    n_mem = kv.shape[0] // batch
    row = lambda bi, l: (bi * nl + l, 0)
    return pl.pallas_call(
        functools.partial(_xattn_kernel, alpha=alpha),
        grid=(batch, nl),
        in_specs=[pl.BlockSpec((tl, d), row),
                  pl.BlockSpec((n_mem, kv.shape[1]), lambda bi, l: (bi, 0)),
                  _const_spec(wq.shape), _const_spec(wo.shape),
                  _const_spec(g.shape), _const_spec(b.shape)],
        out_specs=pl.BlockSpec((tl, d), row),
        out_shape=jax.ShapeDtypeStruct((t, d), F32),
        compiler_params=_params(2),
        name="xattn",
    )(x, kv, wq, wo, g, b)


def _row(v):
    return v.reshape(1, -1).astype(F32)


def _pad_to(v, shape):
    return jnp.pad(v, tuple((0, n - s) for s, n in zip(v.shape, shape)))


def _layer(x, mem, p, *, batch, alpha):
    d = x.shape[1]
    kw = GDN_HEADS * GDN_DK
    vw = GDN_HEADS * GDN_DV
    w_in = p["w_in"]
    o0 = 2 * kw + vw
    o1 = o0 + vw
    o2 = o1 + 2 * GDN_HEADS
    o3 = o2 + 2 * d
    w_ab = w_in[:, o1:o2]
    wab = _pad_to(w_ab, (d, LANES)).astype(BF16)
    wabt = _pad_to(w_ab[:, :GDN_HEADS].T, (BF16_ROWS, d)).astype(BF16)
    alog = _row(p["gdn_a_log"])
    dtb = _row(p["gdn_dt_bias"])

    x, q, k, kb, qg, kd, rhs, gc, gl, gcr = _ffn_prep(
        x, p["ffn1_wg"].astype(BF16), p["ffn1_wu"].astype(BF16), p["ffn1_wd"].astype(BF16),
        _row(p["ln1_g"]), _row(p["ln1_b"]), w_in[:, :o0].astype(BF16), wab, wabt,
        p["gdn_conv_qkv"].astype(F32), _pad_to(alog, (1, LANES)), _pad_to(dtb, (1, LANES)),
        _pad_to(alog.reshape(-1, 1), (BF16_ROWS, 1)), _pad_to(dtb.reshape(-1, 1), (BF16_ROWS, 1)),
        alpha=alpha, batch=batch)
    o, c, sg = _mixer(
        x, q, k, kb, qg, kd, rhs, gc, gl, gcr, _row(p["gdn_norm_g"]),
        w_in[:, o0:o1].astype(BF16), w_in[:, o2:o3].astype(BF16), w_in[:, o3:].astype(BF16),
        p["conv_dw_w"].astype(F32), _row(p["conv_dw_b"]), _row(p["conv_ln_g"]), _row(p["conv_ln_b"]),
        batch=batch)
    x = _merge(x, o, c, sg, p["w_gdn_out"].astype(BF16), p["w_conv_out"].astype(BF16),
               _row(p["b_conv_out"]), p["w_mix_out"].astype(BF16),
               _row(p["ln2_g"]), _row(p["ln2_b"]), alpha=alpha)

    kv = _kv_proj(mem, p["w_xkv"].astype(BF16))
    x = _xattn(x, kv, p["w_xq"].astype(BF16), p["w_xo"].astype(BF16),
               _row(p["ln3_g"]), _row(p["ln3_b"]), alpha=alpha, batch=batch)

    return _ffn_ln(x, p["ffn2_wg"].astype(BF16), p["ffn2_wu"].astype(BF16), p["ffn2_wd"].astype(BF16),
                   _row(p["ln4_g"]), _row(p["ln4_b"]), alpha=alpha)


def kernel(x, mem, ffn1_wg, ffn1_wu, ffn1_wd, ln1_g, ln1_b, w_in, gdn_conv_qkv, gdn_a_log, gdn_dt_bias, gdn_norm_g, w_gdn_out, conv_dw_w, conv_dw_b, conv_ln_g, conv_ln_b, w_conv_out, b_conv_out, w_mix_out, ln2_g, ln2_b, w_xq, w_xkv, w_xo, ln3_g, ln3_b, ffn2_wg, ffn2_wu, ffn2_wd, ln4_g, ln4_b):
    weights = dict(
        ffn1_wg=ffn1_wg, ffn1_wu=ffn1_wu, ffn1_wd=ffn1_wd, ln1_g=ln1_g, ln1_b=ln1_b, w_in=w_in,
        gdn_conv_qkv=gdn_conv_qkv, gdn_a_log=gdn_a_log, gdn_dt_bias=gdn_dt_bias,
        gdn_norm_g=gdn_norm_g, w_gdn_out=w_gdn_out, conv_dw_w=conv_dw_w, conv_dw_b=conv_dw_b,
        conv_ln_g=conv_ln_g, conv_ln_b=conv_ln_b, w_conv_out=w_conv_out, b_conv_out=b_conv_out,
        w_mix_out=w_mix_out, ln2_g=ln2_g, ln2_b=ln2_b, w_xq=w_xq, w_xkv=w_xkv, w_xo=w_xo,
        ln3_g=ln3_g, ln3_b=ln3_b, ffn2_wg=ffn2_wg, ffn2_wu=ffn2_wu, ffn2_wd=ffn2_wd,
        ln4_g=ln4_g, ln4_b=ln4_b)
    batch, seq, d = x.shape
    depth = ffn1_wg.shape[0]
    alpha = (2 * depth) ** 0.25
    h = x.reshape(batch * seq, d)
    m = mem.reshape(-1, d)
    for i in range(depth):
        h = _layer(h, m, {k: v[i] for k, v in weights.items()}, batch=batch, alpha=alpha)
    return h.reshape(batch, seq, d)
```

```python
import functools

import jax
import jax.numpy as jnp
from jax import lax
from jax.experimental import pallas as pl
from jax.experimental.pallas import tpu as pltpu

F32 = jnp.float32
BF16 = jnp.bfloat16

LN_EPS = 1e-5
RMS_EPS = 1e-6
GDN_HEADS = 8
GDN_DK = 128
GDN_DV = 128
SHORT_CONV = 4
CHUNK = 64
CONV_WIDTH = 31
XATTN_HEADS = 4
N_MEM = 256

LANES = 128
SUBLANES = 8
BF16_ROWS = 16
VMEM_LIMIT = 56 * 1024 * 1024
CONV_HALO = 5 * SUBLANES


def _params(n_axes):
    return pltpu.CompilerParams(dimension_semantics=("arbitrary",) * n_axes,
                                vmem_limit_bytes=VMEM_LIMIT)


def _const_spec(shape):
    zeros = (0,) * len(shape)
    return pl.BlockSpec(shape, lambda *_: zeros, pipeline_mode=pl.Buffered(1))


def _mm_nt(a, b):
    return lax.dot_general(a.astype(BF16), b.astype(BF16), (((1,), (1,)), ((), ())),
                           preferred_element_type=F32)


def _split3(v):
    h1 = v.astype(BF16)
    r1 = v - h1.astype(F32)
    h2 = r1.astype(BF16)
    h3 = (r1 - h2.astype(F32)).astype(BF16)
    return h1, h2, h3


def _layer_norm(v, g, b):
    mu = jnp.mean(v, axis=-1, keepdims=True)
    d = v - mu
    var = jnp.mean(d * d, axis=-1, keepdims=True)
    return d * lax.rsqrt(var + LN_EPS) * g + b


def _silu(v):
    return v * jax.nn.sigmoid(v)


def _softplus(v):
    return jnp.maximum(v, 0.0) + jnp.log1p(jnp.exp(-jnp.abs(v)))


def _ffn_residual(x, wg_ref, wu_ref, wd_ref, *, alpha, fc):
    xb = x.astype(BF16)
    acc = jnp.zeros(x.shape, F32)
    for j in range(wg_ref.shape[1] // fc):
        sl = slice(j * fc, (j + 1) * fc)
        gate = jnp.dot(xb, wg_ref[:, sl], preferred_element_type=F32)
        up = jnp.dot(xb, wu_ref[:, sl], preferred_element_type=F32)
        h = (_silu(gate) * up).astype(BF16)
        acc = acc + jnp.dot(h, wd_ref[sl, :], preferred_element_type=F32)
    return alpha * x + 0.5 * acc


def _ffn_ln_kernel(x_ref, wg_ref, wu_ref, wd_ref, g_ref, b_ref, o_ref, *, alpha, fc):
    y = _ffn_residual(x_ref[...], wg_ref, wu_ref, wd_ref, alpha=alpha, fc=fc)
    o_ref[...] = _layer_norm(y, g_ref[...], b_ref[...])


def _ffn_ln(x, wg, wu, wd, g, b, *, alpha, tm=512, fc=256):
    t, d = x.shape
    consts = (wg, wu, wd, g, b)
    return pl.pallas_call(
        functools.partial(_ffn_ln_kernel, alpha=alpha, fc=fc),
        grid=(t // tm,),
        in_specs=[pl.BlockSpec((tm, d), lambda i: (i, 0))] + [_const_spec(w.shape) for w in consts],
        out_specs=pl.BlockSpec((tm, d), lambda i: (i, 0)),
        out_shape=jax.ShapeDtypeStruct((t, d), F32),
        compiler_params=_params(1),
        name="ffn_ln",
    )(x, *consts)


def _ffn_prep_kernel(x_ref, wg_ref, wu_ref, wd_ref, lg_ref, lb_ref, wqkv_ref, wab_ref, wabt_ref, cw_ref,
                     alog_r_ref, dtb_r_ref, alog_c_ref, dtb_c_ref,
                     x1_ref, q_ref, k_ref, kb_ref, qg_ref, kd_ref, rhs_ref, gc_ref, gl_ref, gcr_ref,
                     ext_ref, *, alpha, fc, tl):
    halo = SUBLANES
    kw = GDN_HEADS * GDN_DK
    nc = tl // CHUNK

    @pl.when(pl.program_id(1) == 0)
    def _():
        ext_ref[0:halo, :] = jnp.zeros((halo, ext_ref.shape[1]), F32)

    y = _ffn_residual(x_ref[...], wg_ref, wu_ref, wd_ref, alpha=alpha, fc=fc)
    x1 = _layer_norm(y, lg_ref[...], lb_ref[...])
    x1_ref[...] = x1
    xb = x1.astype(BF16)
    ext_ref[halo:halo + tl, :] = jnp.dot(xb, wqkv_ref[...], preferred_element_type=F32)
    ab = jnp.dot(xb, wab_ref[...], preferred_element_type=F32)
    arow = lax.dot_general(wabt_ref[...], xb, (((1,), (1,)), ((), ())), preferred_element_type=F32)

    row = lax.broadcasted_iota(jnp.int32, (tl, tl), 0)
    col = lax.broadcasted_iota(jnp.int32, (tl, tl), 1)
    same = (row // CHUNK) == (col // CHUNK)
    tri_b = jnp.where(same & (row >= col), 1.0, 0.0).astype(BF16)
    utri_b = jnp.where(same & (row <= col), 1.0, 0.0).astype(BF16)
    g_col = -jnp.exp(alog_r_ref[...]) * _softplus(ab + dtb_r_ref[...])
    gc_col = sum(jnp.dot(tri_b, part, preferred_element_type=F32) for part in _split3(g_col))
    gl_col = jnp.concatenate(
        [jnp.broadcast_to(gc_col[(c + 1) * CHUNK - 1:(c + 1) * CHUNK, :], (CHUNK, LANES))
         for c in range(nc)], axis=0)
    g_row = -jnp.exp(alog_c_ref[...]) * _softplus(arow + dtb_c_ref[...])
    gcr_ref[...] = sum(jnp.dot(part, utri_b, preferred_element_type=F32) for part in _split3(g_row))
    gc_ref[...] = gc_col
    gl_ref[...] = gl_col
    beta_all = jax.nn.sigmoid(ab)
    e_all = jnp.exp(gc_col)
    d_all = jnp.exp(gl_col - gc_col)

    def conv_silu(c0):
        acc = None
        for j in reversed(range(SHORT_CONV)):
            r0 = halo - (SHORT_CONV - 1) + j
            term = cw_ref[j:j + 1, c0:c0 + LANES] * ext_ref[r0:r0 + tl, c0:c0 + LANES]
            acc = term if acc is None else acc + term
        return _silu(acc)

    for h in range(GDN_HEADS):
        hs = slice(h * GDN_DK, (h + 1) * GDN_DK)
        q = conv_silu(h * GDN_DK)
        k = conv_silu(kw + h * GDN_DK)
        v = conv_silu(2 * kw + h * GDN_DV)
        q = q * (lax.rsqrt(jnp.sum(q * q, axis=-1, keepdims=True) + RMS_EPS) * (GDN_DK ** -0.5))
        k = k * lax.rsqrt(jnp.sum(k * k, axis=-1, keepdims=True) + RMS_EPS)
        beta = beta_all[:, GDN_HEADS + h:GDN_HEADS + h + 1]
        e_g = e_all[:, h:h + 1]
        kb = k * beta
        q_ref[:, hs] = q.astype(BF16)
        k_ref[:, hs] = k.astype(BF16)
        kb_ref[:, hs] = kb.astype(BF16)
        qg_ref[:, hs] = (q * e_g).astype(BF16)
        kd_ref[:, hs] = (k * d_all[:, h:h + 1]).astype(BF16)
        rhs_ref[:, 2 * h * GDN_DV:(2 * h + 1) * GDN_DV] = (v * beta).astype(BF16)
        rhs_ref[:, (2 * h + 1) * GDN_DV:(2 * h + 2) * GDN_DV] = (kb * e_g).astype(BF16)

    ext_ref[0:halo, :] = ext_ref[tl:tl + halo, :]


def _ffn_prep(x, wg, wu, wd, lg, lb, wqkv, wab, wabt, cw, alog_r, dtb_r, alog_c, dtb_c,
              *, alpha, batch, tl=256, fc=2816):
    t, d = x.shape
    nl = t // batch // tl
    kw = GDN_HEADS * GDN_DK
    vw = GDN_HEADS * GDN_DV
    nr = wabt.shape[0]
    row = lambda b, l: (b * nl + l, 0)
    consts = (wg, wu, wd, lg, lb, wqkv, wab, wabt, cw, alog_r, dtb_r, alog_c, dtb_c)
    wide = lambda n, dt: (pl.BlockSpec((tl, n), row), jax.ShapeDtypeStruct((t, n), dt))
    outs = [wide(d, F32)] + [wide(kw, BF16)] * 5 + [wide(2 * vw, BF16), wide(LANES, F32), wide(LANES, F32),
                                                    (pl.BlockSpec((nr, tl), lambda b, l: (0, b * nl + l)),
                                                     jax.ShapeDtypeStruct((nr, t), F32))]
    return pl.pallas_call(
        functools.partial(_ffn_prep_kernel, alpha=alpha, fc=fc, tl=tl),
        grid=(batch, nl),
        in_specs=[pl.BlockSpec((tl, d), row)] + [_const_spec(w.shape) for w in consts],
        out_specs=[o[0] for o in outs],
        out_shape=[o[1] for o in outs],
        scratch_shapes=[pltpu.VMEM((SUBLANES + tl, wqkv.shape[1]), F32)],
        compiler_params=_params(2),
        name="ffn_prep",
    )(x, *consts)


def _conv31_cols(w_ref, ext_ref, y_ref, c0, *, tl, rb):
    cols = slice(c0, c0 + LANES)
    for t0 in range(0, tl, rb):
        acc = None
        for r in range(SUBLANES):
            z = None
            for q in range((CONV_WIDTH - 1 - r) // SUBLANES + 1):
                j = CONV_WIDTH - 1 - (SUBLANES * q + r)
                lo = CONV_HALO + t0 - SUBLANES * (q + 1)
                term = w_ref[j:j + 1, cols] * ext_ref[lo:lo + rb + SUBLANES, cols]
                z = term if z is None else z + term
            sh = z[SUBLANES - r:SUBLANES - r + rb]
            acc = sh if acc is None else acc + sh
        y_ref[t0:t0 + rb, cols] = acc


def _mixer_kernel(x_ref, q_ref, k_ref, kb_ref, qg_ref, kd_ref, rhs_ref, gc_ref, gl_ref, gcr_ref,
                  ng_ref, wz_ref, wglu_ref, wgate_ref, cw_ref, cb_ref, lg_ref, lb_ref,
                  o_ref, cact_ref, sg_ref, s_ref, ext_ref, y_ref, *, tl, rb):
    nc = tl // CHUNK
    heads = range(GDN_HEADS)
    ch = ext_ref.shape[1]

    @pl.when(pl.program_id(1) == 0)
    def _():
        s_ref[...] = jnp.zeros(s_ref.shape, F32)
        ext_ref[0:CONV_HALO, :] = jnp.zeros((CONV_HALO, ch), F32)

    xb = x_ref[...].astype(BF16)
    for c0 in range(0, ch, LANES):
        glu = jnp.dot(xb, wglu_ref[:, 2 * c0:2 * c0 + 2 * LANES], preferred_element_type=F32)
        ext_ref[CONV_HALO:CONV_HALO + tl, c0:c0 + LANES] = glu[:, :LANES] * jax.nn.sigmoid(glu[:, LANES:])
        _conv31_cols(cw_ref, ext_ref, y_ref, c0, tl=tl, rb=rb)
    cact_ref[...] = _silu(_layer_norm(y_ref[...] + cb_ref[...], lg_ref[...], lb_ref[...])
                          ).astype(cact_ref.dtype)
    ext_ref[0:CONV_HALO, :] = ext_ref[tl:tl + CONV_HALO, :]
    zs = _silu(jnp.dot(xb, wz_ref[...], preferred_element_type=F32))
    sg_ref[...] = jax.nn.sigmoid(
        jnp.dot(xb, wgate_ref[...], preferred_element_type=F32)).astype(sg_ref.dtype)

    row = lax.broadcasted_iota(jnp.int32, (tl, tl), 0)
    col = lax.broadcasted_iota(jnp.int32, (tl, tl), 1)
    same_b = jnp.where((row // CHUNK) == (col // CHUNK), 1.0, 0.0).astype(BF16)
    pi = lax.broadcasted_iota(jnp.int32, (CHUNK, tl), 0)
    lane = lax.broadcasted_iota(jnp.int32, (CHUNK, tl), 1)
    lc = lane // CHUNK
    lj = lane % CHUNK
    tri_p = pi >= lj
    strict_p = pi > lj
    eye_p = jnp.where(pi == lj, 1.0, 0.0)
    base = 16
    base_p = (pi // base) == (lj // base)

    def pack(full):
        p = full[0:CHUNK]
        for c in range(1, nc):
            p = jnp.where(lc == c, full[c * CHUNK:(c + 1) * CHUNK], p)
        return p

    def blockdiag(p):
        return jnp.concatenate([p.astype(BF16)] * nc, axis=0) * same_b

    def pmm(a, b):
        return jnp.dot(a.astype(BF16), blockdiag(b), preferred_element_type=F32)

    m, a = [], []
    for h in heads:
        hs = slice(h * GDN_DK, (h + 1) * GDN_DK)
        k = k_ref[:, hs]
        g_pack = pack(jnp.broadcast_to(gc_ref[:, h:h + 1], (tl, tl)))
        diff = g_pack - gcr_ref[h:h + 1, :]
        decay = jnp.where(tri_p, jnp.exp(jnp.where(tri_p, diff, 0.0)), 0.0)
        m.append(jnp.where(strict_p, pack(_mm_nt(kb_ref[:, hs], k)) * decay, 0.0))
        a.append(pack(_mm_nt(q_ref[:, hs], k)) * decay)

    n = [jnp.where(base_p, -m[h], 0.0) for h in heads]
    inv = [eye_p + n[h] for h in heads]
    for _ in range(3):
        n = [pmm(n[h], n[h]) for h in heads]
        inv = [inv[h] + pmm(n[h], inv[h]) for h in heads]
    s = base
    while s < CHUNK:
        lower_left = ((pi // (2 * s)) == (lj // (2 * s))) & ((pi // s) == (lj // s) + 1)
        e = [pmm(jnp.where(lower_left, m[h], 0.0), inv[h]) for h in heads]
        inv = [inv[h] - pmm(inv[h], e[h]) for h in heads]
        s *= 2
    xb16 = [jnp.dot(blockdiag(inv[h]), rhs_ref[:, 2 * h * GDN_DV:(2 * h + 2) * GDN_DV],
                    preferred_element_type=F32).astype(BF16) for h in heads]

    ax = [jnp.dot(blockdiag(a[h]), xb16[h], preferred_element_type=F32) for h in heads]
    kx = [[lax.dot_general(kd_ref[c * CHUNK:(c + 1) * CHUNK, h * GDN_DK:(h + 1) * GDN_DK],
                           xb16[h][c * CHUNK:(c + 1) * CHUNK], (((0,), (0,)), ((), ())),
                           preferred_element_type=F32) for c in range(nc)] for h in heads]
    state = [s_ref[h] for h in heads]
    outs = [[] for _ in heads]
    for c in range(nc):
        r = slice(c * CHUNK, (c + 1) * CHUNK)
        for h in heads:
            hs = slice(h * GDN_DK, (h + 1) * GDN_DK)
            sb = state[h].astype(BF16)
            r_c = (qg_ref[r, hs].astype(F32) - ax[h][r, GDN_DV:]).astype(BF16)
            outs[h].append(jnp.dot(r_c, sb, preferred_element_type=F32) + ax[h][r, :GDN_DV])
            state[h] = (state[h] * jnp.exp(gl_ref[c * CHUNK:c * CHUNK + 1, h:h + 1])
                        - jnp.dot(kx[h][c][:, GDN_DV:].astype(BF16), sb, preferred_element_type=F32)
                        + kx[h][c][:, :GDN_DV])

    for h in heads:
        hs = slice(h * GDN_DV, (h + 1) * GDN_DV)
        s_ref[h] = state[h]
        o = jnp.concatenate(outs[h], axis=0)
        o = o * lax.rsqrt(jnp.mean(o * o, axis=-1, keepdims=True) + RMS_EPS) * ng_ref[...]
        o_ref[:, hs] = (o * zs[:, hs]).astype(o_ref.dtype)


def _mixer(x, q, k, kb, qg, kd, rhs, gc, gl, gcr, ng, wz, wglu, wgate, cw, cb, lg, lb,
           *, batch, tl=256, rb=64):
    t = x.shape[0]
    nl = t // batch // tl
    vw, ch, ns = wz.shape[1], wglu.shape[1] // 2, wgate.shape[1]
    row = lambda b, l: (b * nl + l, 0)
    tiles = (x, q, k, kb, qg, kd, rhs, gc, gl)
    consts = (ng, wz, wglu, wgate, cw, cb, lg, lb)
    return pl.pallas_call(
        functools.partial(_mixer_kernel, tl=tl, rb=rb),
        grid=(batch, nl),
        in_specs=([pl.BlockSpec((tl, v.shape[1]), row) for v in tiles]
                  + [pl.BlockSpec((gcr.shape[0], tl), lambda b, l: (0, b * nl + l))]
                  + [_const_spec(w.shape) for w in consts]),
        out_specs=[pl.BlockSpec((tl, vw), row), pl.BlockSpec((tl, ch), row), pl.BlockSpec((tl, ns), row)],
        out_shape=[jax.ShapeDtypeStruct((t, vw), BF16), jax.ShapeDtypeStruct((t, ch), BF16),
                   jax.ShapeDtypeStruct((t, ns), BF16)],
        scratch_shapes=[pltpu.VMEM((GDN_HEADS, GDN_DK, GDN_DV), F32),
                        pltpu.VMEM((CONV_HALO + tl, ch), F32), pltpu.VMEM((tl, ch), F32)],
        compiler_params=_params(2),
        name="mixer",
    )(*tiles, gcr, *consts)


def _merge_kernel(x_ref, o_ref, c_ref, sg_ref, wa_ref, wc_ref, bc_ref, wm_ref, g_ref, b_ref,
                  out_ref, *, alpha):
    d = x_ref.shape[1]
    y_a = jnp.dot(o_ref[...], wa_ref[...], preferred_element_type=F32)
    y_c = jnp.dot(c_ref[...], wc_ref[...], preferred_element_type=F32) + bc_ref[...]
    y = sg_ref[:, :d] * y_a + sg_ref[:, d:] * y_c
    mixed = jnp.dot(y.astype(BF16), wm_ref[...], preferred_element_type=F32)
    out_ref[...] = _layer_norm(alpha * x_ref[...] + mixed, g_ref[...], b_ref[...])


def _merge(x, o, c, sg, wa, wc, bc, wm, g, b, *, alpha, tm=1024):
    t, d = x.shape
    row = lambda i: (i, 0)
    return pl.pallas_call(
        functools.partial(_merge_kernel, alpha=alpha),
        grid=(t // tm,),
        in_specs=[pl.BlockSpec((tm, d), row), pl.BlockSpec((tm, o.shape[1]), row),
                  pl.BlockSpec((tm, c.shape[1]), row), pl.BlockSpec((tm, sg.shape[1]), row),
                  _const_spec(wa.shape), _const_spec(wc.shape), _const_spec(bc.shape),
                  _const_spec(wm.shape), _const_spec(g.shape), _const_spec(b.shape)],
        out_specs=pl.BlockSpec((tm, d), row),
        out_shape=jax.ShapeDtypeStruct((t, d), F32),
        compiler_params=_params(1),
        name="merge",
    )(x, o, c, sg, wa, wc, bc, wm, g, b)


def _kv_proj_kernel(m_ref, w_ref, o_ref):
    o_ref[...] = jnp.dot(m_ref[...].astype(BF16), w_ref[...],
                         preferred_element_type=F32).astype(o_ref.dtype)


def _kv_proj(mem, w, *, tm=256):
    t, d = mem.shape
    n = w.shape[1]
    return pl.pallas_call(
        _kv_proj_kernel,
        grid=(t // tm,),
        in_specs=[pl.BlockSpec((tm, d), lambda i: (i, 0)), _const_spec(w.shape)],
        out_specs=pl.BlockSpec((tm, n), lambda i: (i, 0)),
        out_shape=jax.ShapeDtypeStruct((t, n), BF16),
        compiler_params=_params(1),
        name="kv_proj",
    )(mem, w)


def _xattn_kernel(x_ref, kv_ref, wq_ref, wo_ref, g_ref, b_ref, out_ref, *, alpha):
    x = x_ref[...]
    d = x.shape[1]
    hd = d // XATTN_HEADS
    q = jnp.dot(x.astype(BF16), wq_ref[...], preferred_element_type=F32).astype(BF16)
    heads = []
    for h in range(XATTN_HEADS):
        k = kv_ref[:, h * hd:(h + 1) * hd]
        v = kv_ref[:, d + h * hd:d + (h + 1) * hd]
        s = _mm_nt(q[:, h * hd:(h + 1) * hd], k) * (hd ** -0.5)
        e = jnp.exp(s - jnp.max(s, axis=-1, keepdims=True))
        p = e / jnp.sum(e, axis=-1, keepdims=True)
        heads.append(jnp.dot(p.astype(BF16), v, preferred_element_type=F32).astype(BF16))
    o = jnp.concatenate(heads, axis=1)
    att = jnp.dot(o, wo_ref[...], preferred_element_type=F32)
    out_ref[...] = _layer_norm(alpha * x + att, g_ref[...], b_ref[...])


def _xattn(x, kv, wq, wo, g, b, *, alpha, batch, tl=1024):
    t, d = x.shape
    nl = t // batch // tl
    n_mem = kv.shape[0] // batch
    row = lambda bi, l: (bi * nl + l, 0)
    return pl.pallas_call(
        functools.partial(_xattn_kernel, alpha=alpha),
        grid=(batch, nl),
        in_specs=[pl.BlockSpec((tl, d), row),
                  pl.BlockSpec((n_mem, kv.shape[1]), lambda bi, l: (bi, 0)),
                  _const_spec(wq.shape), _const_spec(wo.shape),
                  _const_spec(g.shape), _const_spec(b.shape)],
        out_specs=pl.BlockSpec((tl, d), row),
        out_shape=jax.ShapeDtypeStruct((t, d), F32),
        compiler_params=_params(2),
        name="xattn",
    )(x, kv, wq, wo, g, b)


def _row(v):
    return v.reshape(1, -1).astype(F32)


def _pad_to(v, shape):
    return jnp.pad(v, tuple((0, n - s) for s, n in zip(v.shape, shape)))


def _interleave_halves(w):
    k, n = w.shape
    return w.reshape(k, 2, n // 2 // LANES, LANES).transpose(0, 2, 1, 3).reshape(k, n)


def _layer(x, mem, p, *, batch, alpha):
    d = x.shape[1]
    kw = GDN_HEADS * GDN_DK
    vw = GDN_HEADS * GDN_DV
    w_in = p["w_in"]
    o0 = 2 * kw + vw
    o1 = o0 + vw
    o2 = o1 + 2 * GDN_HEADS
    o3 = o2 + 2 * d
    w_ab = w_in[:, o1:o2]
    wab = _pad_to(w_ab, (d, LANES)).astype(BF16)
    wabt = _pad_to(w_ab[:, :GDN_HEADS].T, (BF16_ROWS, d)).astype(BF16)
    alog = _row(p["gdn_a_log"])
    dtb = _row(p["gdn_dt_bias"])

    x, q, k, kb, qg, kd, rhs, gc, gl, gcr = _ffn_prep(
        x, p["ffn1_wg"].astype(BF16), p["ffn1_wu"].astype(BF16), p["ffn1_wd"].astype(BF16),
        _row(p["ln1_g"]), _row(p["ln1_b"]), w_in[:, :o0].astype(BF16), wab, wabt,
        p["gdn_conv_qkv"].astype(F32), _pad_to(alog, (1, LANES)), _pad_to(dtb, (1, LANES)),
        _pad_to(alog.reshape(-1, 1), (BF16_ROWS, 1)), _pad_to(dtb.reshape(-1, 1), (BF16_ROWS, 1)),
        alpha=alpha, batch=batch)
    o, c, sg = _mixer(
        x, q, k, kb, qg, kd, rhs, gc, gl, gcr, _row(p["gdn_norm_g"]),
        w_in[:, o0:o1].astype(BF16), _interleave_halves(w_in[:, o2:o3]).astype(BF16),
        w_in[:, o3:].astype(BF16),
        p["conv_dw_w"].astype(F32), _row(p["conv_dw_b"]), _row(p["conv_ln_g"]), _row(p["conv_ln_b"]),
        batch=batch)
    x = _merge(x, o, c, sg, p["w_gdn_out"].astype(BF16), p["w_conv_out"].astype(BF16),
               _row(p["b_conv_out"]), p["w_mix_out"].astype(BF16),
               _row(p["ln2_g"]), _row(p["ln2_b"]), alpha=alpha)

    kv = _kv_proj(mem, p["w_xkv"].astype(BF16))
    x = _xattn(x, kv, p["w_xq"].astype(BF16), p["w_xo"].astype(BF16),
               _row(p["ln3_g"]), _row(p["ln3_b"]), alpha=alpha, batch=batch)

    return _ffn_ln(x, p["ffn2_wg"].astype(BF16), p["ffn2_wu"].astype(BF16), p["ffn2_wd"].astype(BF16),
                   _row(p["ln4_g"]), _row(p["ln4_b"]), alpha=alpha)


def kernel(x, mem, ffn1_wg, ffn1_wu, ffn1_wd, ln1_g, ln1_b, w_in, gdn_conv_qkv, gdn_a_log, gdn_dt_bias, gdn_norm_g, w_gdn_out, conv_dw_w, conv_dw_b, conv_ln_g, conv_ln_b, w_conv_out, b_conv_out, w_mix_out, ln2_g, ln2_b, w_xq, w_xkv, w_xo, ln3_g, ln3_b, ffn2_wg, ffn2_wu, ffn2_wd, ln4_g, ln4_b):
    weights = dict(
        ffn1_wg=ffn1_wg, ffn1_wu=ffn1_wu, ffn1_wd=ffn1_wd, ln1_g=ln1_g, ln1_b=ln1_b, w_in=w_in,
        gdn_conv_qkv=gdn_conv_qkv, gdn_a_log=gdn_a_log, gdn_dt_bias=gdn_dt_bias,
        gdn_norm_g=gdn_norm_g, w_gdn_out=w_gdn_out, conv_dw_w=conv_dw_w, conv_dw_b=conv_dw_b,
        conv_ln_g=conv_ln_g, conv_ln_b=conv_ln_b, w_conv_out=w_conv_out, b_conv_out=b_conv_out,
        w_mix_out=w_mix_out, ln2_g=ln2_g, ln2_b=ln2_b, w_xq=w_xq, w_xkv=w_xkv, w_xo=w_xo,
        ln3_g=ln3_g, ln3_b=ln3_b, ffn2_wg=ffn2_wg, ffn2_wu=ffn2_wu, ffn2_wd=ffn2_wd,
        ln4_g=ln4_g, ln4_b=ln4_b)
    batch, seq, d = x.shape
    depth = ffn1_wg.shape[0]
    alpha = (2 * depth) ** 0.25
    h = x.reshape(batch * seq, d)
    m = mem.reshape(-1, d)
    for i in range(depth):
        h = _layer(h, m, {k: v[i] for k, v in weights.items()}, batch=batch, alpha=alpha)
    return h.reshape(batch, seq, d)
```

```python
import functools

import jax
import jax.numpy as jnp
from jax import lax
from jax.experimental import pallas as pl
from jax.experimental.pallas import tpu as pltpu

F32 = jnp.float32
BF16 = jnp.bfloat16

LN_EPS = 1e-5
RMS_EPS = 1e-6
GDN_HEADS = 8
GDN_DK = 128
GDN_DV = 128
SHORT_CONV = 4
CHUNK = 64
CONV_WIDTH = 31
XATTN_HEADS = 4
N_MEM = 256

LANES = 128
SUBLANES = 8
BF16_ROWS = 16
VMEM_LIMIT = 56 * 1024 * 1024
CONV_HALO = 5 * SUBLANES


def _params(n_axes):
    return pltpu.CompilerParams(dimension_semantics=("arbitrary",) * n_axes,
                                vmem_limit_bytes=VMEM_LIMIT)


def _const_spec(shape):
    zeros = (0,) * len(shape)
    return pl.BlockSpec(shape, lambda *_: zeros, pipeline_mode=pl.Buffered(1))


def _mm_nt(a, b):
    return lax.dot_general(a.astype(BF16), b.astype(BF16), (((1,), (1,)), ((), ())),
                           preferred_element_type=F32)


def _split3(v):
    h1 = v.astype(BF16)
    r1 = v - h1.astype(F32)
    h2 = r1.astype(BF16)
    h3 = (r1 - h2.astype(F32)).astype(BF16)
    return h1, h2, h3


def _layer_norm(v, g, b):
    mu = jnp.mean(v, axis=-1, keepdims=True)
    d = v - mu
    var = jnp.mean(d * d, axis=-1, keepdims=True)
    return d * lax.rsqrt(var + LN_EPS) * g + b


def _silu(v):
    return v * jax.nn.sigmoid(v)


def _softplus(v):
    return jnp.maximum(v, 0.0) + jnp.log1p(jnp.exp(-jnp.abs(v)))


def _ffn_residual(x, wg_ref, wu_ref, wd_ref, *, alpha, fc):
    xb = x.astype(BF16)
    acc = jnp.zeros(x.shape, F32)
    for j in range(wg_ref.shape[1] // fc):
        sl = slice(j * fc, (j + 1) * fc)
        gate = jnp.dot(xb, wg_ref[:, sl].astype(BF16), preferred_element_type=F32)
        up = jnp.dot(xb, wu_ref[:, sl].astype(BF16), preferred_element_type=F32)
        h = (_silu(gate) * up).astype(BF16)
        acc = acc + jnp.dot(h, wd_ref[sl, :].astype(BF16), preferred_element_type=F32)
    return alpha * x + 0.5 * acc


def _ffn_ln_kernel(x_ref, wg_ref, wu_ref, wd_ref, g_ref, b_ref, o_ref, *, alpha, fc):
    y = _ffn_residual(x_ref[...], wg_ref, wu_ref, wd_ref, alpha=alpha, fc=fc)
    o_ref[...] = _layer_norm(y, g_ref[...], b_ref[...])


def _ffn_ln(x, wg, wu, wd, g, b, *, alpha, tm=512, fc=256):
    t, d = x.shape
    consts = (wg, wu, wd, g, b)
    return pl.pallas_call(
        functools.partial(_ffn_ln_kernel, alpha=alpha, fc=fc),
        grid=(t // tm,),
        in_specs=[pl.BlockSpec((tm, d), lambda i: (i, 0))] + [_const_spec(w.shape) for w in consts],
        out_specs=pl.BlockSpec((tm, d), lambda i: (i, 0)),
        out_shape=jax.ShapeDtypeStruct((t, d), F32),
        compiler_params=_params(1),
        name="ffn_ln",
    )(x, *consts)


def _ffn_prep_kernel(x_ref, wg_ref, wu_ref, wd_ref, lg_ref, lb_ref, wqkv_ref, wab_ref, wabt_ref, cw_ref,
                     alog_r_ref, dtb_r_ref, alog_c_ref, dtb_c_ref,
                     x1_ref, q_ref, k_ref, kb_ref, qg_ref, kd_ref, rhs_ref, gc_ref, gl_ref, gcr_ref,
                     ext_ref, *, alpha, fc, tl):
    halo = SUBLANES
    kw = GDN_HEADS * GDN_DK
    nc = tl // CHUNK

    @pl.when(pl.program_id(1) == 0)
    def _():
        ext_ref[0:halo, :] = jnp.zeros((halo, ext_ref.shape[1]), F32)

    y = _ffn_residual(x_ref[...], wg_ref, wu_ref, wd_ref, alpha=alpha, fc=fc)
    x1 = _layer_norm(y, lg_ref[...], lb_ref[...])
    x1_ref[...] = x1
    xb = x1.astype(BF16)
    ext_ref[halo:halo + tl, :] = jnp.dot(xb, wqkv_ref[...], preferred_element_type=F32)
    ab = jnp.dot(xb, wab_ref[...], preferred_element_type=F32)
    arow = lax.dot_general(wabt_ref[...], xb, (((1,), (1,)), ((), ())), preferred_element_type=F32)

    row = lax.broadcasted_iota(jnp.int32, (tl, tl), 0)
    col = lax.broadcasted_iota(jnp.int32, (tl, tl), 1)
    same = (row // CHUNK) == (col // CHUNK)
    tri_b = jnp.where(same & (row >= col), 1.0, 0.0).astype(BF16)
    utri_b = jnp.where(same & (row <= col), 1.0, 0.0).astype(BF16)
    g_col = -jnp.exp(alog_r_ref[...]) * _softplus(ab + dtb_r_ref[...])
    gc_col = sum(jnp.dot(tri_b, part, preferred_element_type=F32) for part in _split3(g_col))
    gl_col = jnp.concatenate(
        [jnp.broadcast_to(gc_col[(c + 1) * CHUNK - 1:(c + 1) * CHUNK, :], (CHUNK, LANES))
         for c in range(nc)], axis=0)
    g_row = -jnp.exp(alog_c_ref[...]) * _softplus(arow + dtb_c_ref[...])
    gcr_ref[...] = sum(jnp.dot(part, utri_b, preferred_element_type=F32) for part in _split3(g_row))
    gc_ref[...] = gc_col
    gl_ref[...] = gl_col
    beta_all = jax.nn.sigmoid(ab)
    e_all = jnp.exp(gc_col)
    d_all = jnp.exp(gl_col - gc_col)

    def conv_silu(c0):
        acc = None
        for j in reversed(range(SHORT_CONV)):
            r0 = halo - (SHORT_CONV - 1) + j
            term = cw_ref[j:j + 1, c0:c0 + LANES] * ext_ref[r0:r0 + tl, c0:c0 + LANES]
            acc = term if acc is None else acc + term
        return _silu(acc)

    for h in range(GDN_HEADS):
        hs = slice(h * GDN_DK, (h + 1) * GDN_DK)
        q = conv_silu(h * GDN_DK)
        k = conv_silu(kw + h * GDN_DK)
        v = conv_silu(2 * kw + h * GDN_DV)
        q = q * (lax.rsqrt(jnp.sum(q * q, axis=-1, keepdims=True) + RMS_EPS) * (GDN_DK ** -0.5))
        k = k * lax.rsqrt(jnp.sum(k * k, axis=-1, keepdims=True) + RMS_EPS)
        beta = beta_all[:, GDN_HEADS + h:GDN_HEADS + h + 1]
        e_g = e_all[:, h:h + 1]
        kb = k * beta
        q_ref[:, hs] = q.astype(BF16)
        k_ref[:, hs] = k.astype(BF16)
        kb_ref[:, hs] = kb.astype(BF16)
        qg_ref[:, hs] = (q * e_g).astype(BF16)
        kd_ref[:, hs] = (k * d_all[:, h:h + 1]).astype(BF16)
        rhs_ref[:, 2 * h * GDN_DV:(2 * h + 1) * GDN_DV] = (v * beta).astype(BF16)
        rhs_ref[:, (2 * h + 1) * GDN_DV:(2 * h + 2) * GDN_DV] = (kb * e_g).astype(BF16)

    ext_ref[0:halo, :] = ext_ref[tl:tl + halo, :]


def _ffn_prep(x, wg, wu, wd, lg, lb, wqkv, wab, wabt, cw, alog_r, dtb_r, alog_c, dtb_c,
              *, alpha, batch, tl=256, fc=2816):
    t, d = x.shape
    nl = t // batch // tl
    kw = GDN_HEADS * GDN_DK
    vw = GDN_HEADS * GDN_DV
    nr = wabt.shape[0]
    row = lambda b, l: (b * nl + l, 0)
    consts = (wg, wu, wd, lg, lb, wqkv, wab, wabt, cw, alog_r, dtb_r, alog_c, dtb_c)
    wide = lambda n, dt: (pl.BlockSpec((tl, n), row), jax.ShapeDtypeStruct((t, n), dt))
    outs = [wide(d, F32)] + [wide(kw, BF16)] * 5 + [wide(2 * vw, BF16), wide(LANES, F32), wide(LANES, F32),
                                                    (pl.BlockSpec((nr, tl), lambda b, l: (0, b * nl + l)),
                                                     jax.ShapeDtypeStruct((nr, t), F32))]
    return pl.pallas_call(
        functools.partial(_ffn_prep_kernel, alpha=alpha, fc=fc, tl=tl),
        grid=(batch, nl),
        in_specs=[pl.BlockSpec((tl, d), row)] + [_const_spec(w.shape) for w in consts],
        out_specs=[o[0] for o in outs],
        out_shape=[o[1] for o in outs],
        scratch_shapes=[pltpu.VMEM((SUBLANES + tl, wqkv.shape[1]), F32)],
        compiler_params=_params(2),
        name="ffn_prep",
    )(x, *consts)


def _conv31_block(w_ref, ext_ref, y_ref, c0, t0, *, rb, anchor=None):
    cols = slice(c0, c0 + LANES)
    acc = None
    for r in range(SUBLANES):
        z = None
        for q in range((CONV_WIDTH - 1 - r) // SUBLANES + 1):
            j = CONV_WIDTH - 1 - (SUBLANES * q + r)
            lo = CONV_HALO + t0 - SUBLANES * (q + 1)
            w = w_ref[j:j + 1, cols]
            if anchor is not None:
                w = jnp.where(anchor[0], w, anchor[1])
            term = w * ext_ref[lo:lo + rb + SUBLANES, cols]
            z = term if z is None else z + term
        sh = z[SUBLANES - r:SUBLANES - r + rb]
        acc = sh if acc is None else acc + sh
    y_ref[t0:t0 + rb, cols] = acc


def _mixer_kernel(x_ref, q_ref, k_ref, kb_ref, qg_ref, kd_ref, rhs_ref, gc_ref, gl_ref, gcr_ref,
                  ng_ref, wz_ref, wglu_ref, wgate_ref, cw_ref, cb_ref, lg_ref, lb_ref, ones_ref,
                  o_ref, cact_ref, sg_ref, s_ref, ext_ref, y_ref, *, tl, rb, anchor_every):
    nc = tl // CHUNK
    heads = range(GDN_HEADS)
    ch = ext_ref.shape[1]

    @pl.when(pl.program_id(1) == 0)
    def _():
        s_ref[...] = jnp.zeros(s_ref.shape, F32)
        ext_ref[0:CONV_HALO, :] = jnp.zeros((CONV_HALO, ch), F32)

    xb = x_ref[...].astype(BF16)
    for c0 in range(0, ch, LANES):
        glu = jnp.dot(xb, wglu_ref[:, 2 * c0:2 * c0 + 2 * LANES], preferred_element_type=F32)
        ext_ref[CONV_HALO:CONV_HALO + tl, c0:c0 + LANES] = glu[:, :LANES] * jax.nn.sigmoid(glu[:, LANES:])
    blocks = [(c0, t0) for c0 in range(0, ch, LANES) for t0 in range(0, tl, rb)]
    late = blocks[len(blocks) // 2:]
    for c0, t0 in blocks[:len(blocks) // 2]:
        _conv31_block(cw_ref, ext_ref, y_ref, c0, t0, rb=rb)
    always = ones_ref[...] > 0.5
    ticks = [0]

    def tick(val):
        ticks[0] += 1
        if late and ticks[0] % anchor_every == 0:
            c0, t0 = late.pop(0)
            _conv31_block(cw_ref, ext_ref, y_ref, c0, t0, rb=rb, anchor=(always, val[0:1, 0:LANES]))
        return val

    zs = _silu(jnp.dot(xb, wz_ref[...], preferred_element_type=F32))
    sg_ref[...] = jax.nn.sigmoid(
        jnp.dot(xb, wgate_ref[...], preferred_element_type=F32)).astype(sg_ref.dtype)

    row = lax.broadcasted_iota(jnp.int32, (tl, tl), 0)
    col = lax.broadcasted_iota(jnp.int32, (tl, tl), 1)
    same_b = jnp.where((row // CHUNK) == (col // CHUNK), 1.0, 0.0).astype(BF16)
    pi = lax.broadcasted_iota(jnp.int32, (CHUNK, tl), 0)
    lane = lax.broadcasted_iota(jnp.int32, (CHUNK, tl), 1)
    lc = lane // CHUNK
    lj = lane % CHUNK
    tri_p = pi >= lj
    strict_p = pi > lj
    eye_p = jnp.where(pi == lj, 1.0, 0.0)
    base = 16
    base_p = (pi // base) == (lj // base)

    def pack(full):
        p = full[0:CHUNK]
        for c in range(1, nc):
            p = jnp.where(lc == c, full[c * CHUNK:(c + 1) * CHUNK], p)
        return p

    def blockdiag(p):
        return jnp.concatenate([p.astype(BF16)] * nc, axis=0) * same_b

    def pmm(a, b):
        return jnp.dot(a.astype(BF16), blockdiag(b), preferred_element_type=F32)

    m, a = [], []
    for h in heads:
        hs = slice(h * GDN_DK, (h + 1) * GDN_DK)
        k = k_ref[:, hs]
        g_pack = pack(jnp.broadcast_to(gc_ref[:, h:h + 1], (tl, tl)))
        diff = g_pack - gcr_ref[h:h + 1, :]
        decay = jnp.where(tri_p, jnp.exp(jnp.where(tri_p, diff, 0.0)), 0.0)
        m.append(jnp.where(strict_p, pack(_mm_nt(kb_ref[:, hs], k)) * decay, 0.0))
        a.append(tick(pack(_mm_nt(q_ref[:, hs], k)) * decay))

    n = [jnp.where(base_p, -m[h], 0.0) for h in heads]
    inv = [eye_p + n[h] for h in heads]
    for _ in range(3):
        n = [tick(pmm(n[h], n[h])) for h in heads]
        inv = [tick(inv[h] + pmm(n[h], inv[h])) for h in heads]
    s = base
    while s < CHUNK:
        lower_left = ((pi // (2 * s)) == (lj // (2 * s))) & ((pi // s) == (lj // s) + 1)
        e = [tick(pmm(jnp.where(lower_left, m[h], 0.0), inv[h])) for h in heads]
        inv = [tick(inv[h] - pmm(inv[h], e[h])) for h in heads]
        s *= 2
    xb16 = [jnp.dot(blockdiag(inv[h]), rhs_ref[:, 2 * h * GDN_DV:(2 * h + 2) * GDN_DV],
                    preferred_element_type=F32).astype(BF16) for h in heads]

    ax = [tick(jnp.dot(blockdiag(a[h]), xb16[h], preferred_element_type=F32)) for h in heads]
    kx = [[lax.dot_general(kd_ref[c * CHUNK:(c + 1) * CHUNK, h * GDN_DK:(h + 1) * GDN_DK],
                           xb16[h][c * CHUNK:(c + 1) * CHUNK], (((0,), (0,)), ((), ())),
                           preferred_element_type=F32) for c in range(nc)] for h in heads]
    state = [s_ref[h] for h in heads]
    outs = [[] for _ in heads]
    for c in range(nc):
        r = slice(c * CHUNK, (c + 1) * CHUNK)
        for h in heads:
            hs = slice(h * GDN_DK, (h + 1) * GDN_DK)
            sb = state[h].astype(BF16)
            r_c = (qg_ref[r, hs].astype(F32) - ax[h][r, GDN_DV:]).astype(BF16)
            outs[h].append(jnp.dot(r_c, sb, preferred_element_type=F32) + ax[h][r, :GDN_DV])
            state[h] = (state[h] * jnp.exp(gl_ref[c * CHUNK:c * CHUNK + 1, h:h + 1])
                        - jnp.dot(kx[h][c][:, GDN_DV:].astype(BF16), sb, preferred_element_type=F32)
                        + kx[h][c][:, :GDN_DV])
            tick(state[h])

    for c0, t0 in late:
        _conv31_block(cw_ref, ext_ref, y_ref, c0, t0, rb=rb)
    cact_ref[...] = _silu(_layer_norm(y_ref[...] + cb_ref[...], lg_ref[...], lb_ref[...])
                          ).astype(cact_ref.dtype)
    ext_ref[0:CONV_HALO, :] = ext_ref[tl:tl + CONV_HALO, :]
    for h in heads:
        hs = slice(h * GDN_DV, (h + 1) * GDN_DV)
        s_ref[h] = state[h]
        o = jnp.concatenate(outs[h], axis=0)
        o = o * lax.rsqrt(jnp.mean(o * o, axis=-1, keepdims=True) + RMS_EPS) * ng_ref[...]
        o_ref[:, hs] = (o * zs[:, hs]).astype(o_ref.dtype)


def _mixer(x, q, k, kb, qg, kd, rhs, gc, gl, gcr, ng, wz, wglu, wgate, cw, cb, lg, lb,
           *, batch, tl=256, rb=64, anchor_every=7):
    t = x.shape[0]
    nl = t // batch // tl
    vw, ch, ns = wz.shape[1], wglu.shape[1] // 2, wgate.shape[1]
    row = lambda b, l: (b * nl + l, 0)
    tiles = (x, q, k, kb, qg, kd, rhs, gc, gl)
    consts = (ng, wz, wglu, wgate, cw, cb, lg, lb, jnp.ones((1, LANES), F32))
    return pl.pallas_call(
        functools.partial(_mixer_kernel, tl=tl, rb=rb, anchor_every=anchor_every),
        grid=(batch, nl),
        in_specs=([pl.BlockSpec((tl, v.shape[1]), row) for v in tiles]
                  + [pl.BlockSpec((gcr.shape[0], tl), lambda b, l: (0, b * nl + l))]
                  + [_const_spec(w.shape) for w in consts]),
        out_specs=[pl.BlockSpec((tl, vw), row), pl.BlockSpec((tl, ch), row), pl.BlockSpec((tl, ns), row)],
        out_shape=[jax.ShapeDtypeStruct((t, vw), BF16), jax.ShapeDtypeStruct((t, ch), BF16),
                   jax.ShapeDtypeStruct((t, ns), BF16)],
        scratch_shapes=[pltpu.VMEM((GDN_HEADS, GDN_DK, GDN_DV), F32),
                        pltpu.VMEM((CONV_HALO + tl, ch), F32), pltpu.VMEM((tl, ch), F32)],
        compiler_params=_params(2),
        name="mixer",
    )(*tiles, gcr, *consts)


def _merge_kernel(x_ref, o_ref, c_ref, sg_ref, wa_ref, wc_ref, bc_ref, wm_ref, g_ref, b_ref,
                  out_ref, *, alpha):
    d = x_ref.shape[1]
    y_a = jnp.dot(o_ref[...], wa_ref[...].astype(BF16), preferred_element_type=F32)
    y_c = jnp.dot(c_ref[...], wc_ref[...].astype(BF16), preferred_element_type=F32) + bc_ref[...]
    y = sg_ref[:, :d] * y_a + sg_ref[:, d:] * y_c
    mixed = jnp.dot(y.astype(BF16), wm_ref[...].astype(BF16), preferred_element_type=F32)
    out_ref[...] = _layer_norm(alpha * x_ref[...] + mixed, g_ref[...], b_ref[...])


def _merge(x, o, c, sg, wa, wc, bc, wm, g, b, *, alpha, tm=1024):
    t, d = x.shape
    row = lambda i: (i, 0)
    return pl.pallas_call(
        functools.partial(_merge_kernel, alpha=alpha),
        grid=(t // tm,),
        in_specs=[pl.BlockSpec((tm, d), row), pl.BlockSpec((tm, o.shape[1]), row),
                  pl.BlockSpec((tm, c.shape[1]), row), pl.BlockSpec((tm, sg.shape[1]), row),
                  _const_spec(wa.shape), _const_spec(wc.shape), _const_spec(bc.shape),
                  _const_spec(wm.shape), _const_spec(g.shape), _const_spec(b.shape)],
        out_specs=pl.BlockSpec((tm, d), row),
        out_shape=jax.ShapeDtypeStruct((t, d), F32),
        compiler_params=_params(1),
        name="merge",
    )(x, o, c, sg, wa, wc, bc, wm, g, b)


def _kv_proj_kernel(m_ref, w_ref, o_ref):
    o_ref[...] = jnp.dot(m_ref[...].astype(BF16), w_ref[...].astype(BF16),
                         preferred_element_type=F32).astype(o_ref.dtype)


def _kv_proj(mem, w, *, tm=256):
    t, d = mem.shape
    n = w.shape[1]
    return pl.pallas_call(
        _kv_proj_kernel,
        grid=(t // tm,),
        in_specs=[pl.BlockSpec((tm, d), lambda i: (i, 0)), _const_spec(w.shape)],
        out_specs=pl.BlockSpec((tm, n), lambda i: (i, 0)),
        out_shape=jax.ShapeDtypeStruct((t, n), BF16),
        compiler_params=_params(1),
        name="kv_proj",
    )(mem, w)


def _xattn_kernel(x_ref, kv_ref, wq_ref, wo_ref, g_ref, b_ref, out_ref, *, alpha):
    x = x_ref[...]
    d = x.shape[1]
    hd = d // XATTN_HEADS
    q = jnp.dot(x.astype(BF16), wq_ref[...].astype(BF16), preferred_element_type=F32).astype(BF16)
    heads = []
    for h in range(XATTN_HEADS):
        k = kv_ref[:, h * hd:(h + 1) * hd]
        v = kv_ref[:, d + h * hd:d + (h + 1) * hd]
        s = _mm_nt(q[:, h * hd:(h + 1) * hd], k) * (hd ** -0.5)
        e = jnp.exp(s - jnp.max(s, axis=-1, keepdims=True))
        p = e / jnp.sum(e, axis=-1, keepdims=True)
        heads.append(jnp.dot(p.astype(BF16), v, preferred_element_type=F32).astype(BF16))
    o = jnp.concatenate(heads, axis=1)
    att = jnp.dot(o, wo_ref[...].astype(BF16), preferred_element_type=F32)
    out_ref[...] = _layer_norm(alpha * x + att, g_ref[...], b_ref[...])


def _xattn(x, kv, wq, wo, g, b, *, alpha, batch, tl=1024):
    t, d = x.shape
    nl = t // batch // tl
    n_mem = kv.shape[0] // batch
    row = lambda bi, l: (bi * nl + l, 0)
    return pl.pallas_call(
        functools.partial(_xattn_kernel, alpha=alpha),
        grid=(batch, nl),
        in_specs=[pl.BlockSpec((tl, d), row),
                  pl.BlockSpec((n_mem, kv.shape[1]), lambda bi, l: (bi, 0)),
                  _const_spec(wq.shape), _const_spec(wo.shape),
                  _const_spec(g.shape), _const_spec(b.shape)],
        out_specs=pl.BlockSpec((tl, d), row),
        out_shape=jax.ShapeDtypeStruct((t, d), F32),
        compiler_params=_params(2),
        name="xattn",
    )(x, kv, wq, wo, g, b)


def _row(v):
    return v.reshape(1, -1).astype(F32)


def _pad_to(v, shape):
    return jnp.pad(v, tuple((0, n - s) for s, n in zip(v.shape, shape)))


def _interleave_halves(w):
    k, n = w.shape
    return w.reshape(k, 2, n // 2 // LANES, LANES).transpose(0, 2, 1, 3).reshape(k, n)


def _layer(x, mem, p, *, batch, alpha):
    d = x.shape[1]
    kw = GDN_HEADS * GDN_DK
    vw = GDN_HEADS * GDN_DV
    w_in = p["w_in"]
    o0 = 2 * kw + vw
    o1 = o0 + vw
    o2 = o1 + 2 * GDN_HEADS
    o3 = o2 + 2 * d
    w_ab = w_in[:, o1:o2]
    wab = _pad_to(w_ab, (d, LANES)).astype(BF16)
    wabt = _pad_to(w_ab[:, :GDN_HEADS].T, (BF16_ROWS, d)).astype(BF16)
    alog = _row(p["gdn_a_log"])
    dtb = _row(p["gdn_dt_bias"])

    x, q, k, kb, qg, kd, rhs, gc, gl, gcr = _ffn_prep(
        x, p["ffn1_wg"].astype(BF16), p["ffn1_wu"].astype(BF16), p["ffn1_wd"].astype(BF16),
        _row(p["ln1_g"]), _row(p["ln1_b"]), w_in[:, :o0].astype(BF16), wab, wabt,
        p["gdn_conv_qkv"].astype(F32), _pad_to(alog, (1, LANES)), _pad_to(dtb, (1, LANES)),
        _pad_to(alog.reshape(-1, 1), (BF16_ROWS, 1)), _pad_to(dtb.reshape(-1, 1), (BF16_ROWS, 1)),
        alpha=alpha, batch=batch)
    o, c, sg = _mixer(
        x, q, k, kb, qg, kd, rhs, gc, gl, gcr, _row(p["gdn_norm_g"]),
        w_in[:, o0:o1].astype(BF16), _interleave_halves(w_in[:, o2:o3]).astype(BF16),
        w_in[:, o3:].astype(BF16),
        p["conv_dw_w"].astype(F32), _row(p["conv_dw_b"]), _row(p["conv_ln_g"]), _row(p["conv_ln_b"]),
        batch=batch)
    x = _merge(x, o, c, sg, p["w_gdn_out"], p["w_conv_out"], _row(p["b_conv_out"]), p["w_mix_out"],
               _row(p["ln2_g"]), _row(p["ln2_b"]), alpha=alpha)

    kv = _kv_proj(mem, p["w_xkv"])
    x = _xattn(x, kv, p["w_xq"], p["w_xo"], _row(p["ln3_g"]), _row(p["ln3_b"]), alpha=alpha, batch=batch)

    return _ffn_ln(x, p["ffn2_wg"], p["ffn2_wu"], p["ffn2_wd"],
                   _row(p["ln4_g"]), _row(p["ln4_b"]), alpha=alpha)


def kernel(x, mem, ffn1_wg, ffn1_wu, ffn1_wd, ln1_g, ln1_b, w_in, gdn_conv_qkv, gdn_a_log, gdn_dt_bias, gdn_norm_g, w_gdn_out, conv_dw_w, conv_dw_b, conv_ln_g, conv_ln_b, w_conv_out, b_conv_out, w_mix_out, ln2_g, ln2_b, w_xq, w_xkv, w_xo, ln3_g, ln3_b, ffn2_wg, ffn2_wu, ffn2_wd, ln4_g, ln4_b):
    weights = dict(
        ffn1_wg=ffn1_wg, ffn1_wu=ffn1_wu, ffn1_wd=ffn1_wd, ln1_g=ln1_g, ln1_b=ln1_b, w_in=w_in,
        gdn_conv_qkv=gdn_conv_qkv, gdn_a_log=gdn_a_log, gdn_dt_bias=gdn_dt_bias,
        gdn_norm_g=gdn_norm_g, w_gdn_out=w_gdn_out, conv_dw_w=conv_dw_w, conv_dw_b=conv_dw_b,
        conv_ln_g=conv_ln_g, conv_ln_b=conv_ln_b, w_conv_out=w_conv_out, b_conv_out=b_conv_out,
        w_mix_out=w_mix_out, ln2_g=ln2_g, ln2_b=ln2_b, w_xq=w_xq, w_xkv=w_xkv, w_xo=w_xo,
        ln3_g=ln3_g, ln3_b=ln3_b, ffn2_wg=ffn2_wg, ffn2_wu=ffn2_wu, ffn2_wd=ffn2_wd,
        ln4_g=ln4_g, ln4_b=ln4_b)
    batch, seq, d = x.shape
    depth = ffn1_wg.shape[0]
    alpha = (2 * depth) ** 0.25
    h = x.reshape(batch * seq, d)
    m = mem.reshape(-1, d)
    for i in range(depth):
        h = _layer(h, m, {k: v[i] for k, v in weights.items()}, batch=batch, alpha=alpha)
    return h.reshape(batch, seq, d)
```

```python
import functools

import jax
import jax.numpy as jnp
from jax import lax
from jax.experimental import pallas as pl
from jax.experimental.pallas import tpu as pltpu

F32 = jnp.float32
BF16 = jnp.bfloat16

LN_EPS = 1e-5
RMS_EPS = 1e-6
GDN_HEADS = 8
GDN_DK = 128
GDN_DV = 128
SHORT_CONV = 4
CHUNK = 64
CONV_WIDTH = 31
XATTN_HEADS = 4
N_MEM = 256

LANES = 128
SUBLANES = 8
BF16_ROWS = 16
VMEM_LIMIT = 56 * 1024 * 1024
CONV_HALO = 5 * SUBLANES


def _params(n_axes):
    return pltpu.CompilerParams(dimension_semantics=("arbitrary",) * n_axes,
                                vmem_limit_bytes=VMEM_LIMIT)


def _const_spec(shape):
    zeros = (0,) * len(shape)
    return pl.BlockSpec(shape, lambda *_: zeros, pipeline_mode=pl.Buffered(1))


def _mm_nt(a, b):
    return lax.dot_general(a.astype(BF16), b.astype(BF16), (((1,), (1,)), ((), ())),
                           preferred_element_type=F32)


def _split3(v):
    h1 = v.astype(BF16)
    r1 = v - h1.astype(F32)
    h2 = r1.astype(BF16)
    h3 = (r1 - h2.astype(F32)).astype(BF16)
    return h1, h2, h3


def _layer_norm(v, g, b):
    mu = jnp.mean(v, axis=-1, keepdims=True)
    d = v - mu
    var = jnp.mean(d * d, axis=-1, keepdims=True)
    return d * lax.rsqrt(var + LN_EPS) * g + b


def _silu(v):
    return v * jax.nn.sigmoid(v)


def _softplus(v):
    return jnp.maximum(v, 0.0) + jnp.log1p(jnp.exp(-jnp.abs(v)))


def _ffn_residual(x, wg_ref, wu_ref, wd_ref, *, alpha, fc):
    xb = x.astype(BF16)
    acc = jnp.zeros(x.shape, F32)
    for j in range(wg_ref.shape[1] // fc):
        sl = slice(j * fc, (j + 1) * fc)
        gate = jnp.dot(xb, wg_ref[:, sl].astype(BF16), preferred_element_type=F32)
        up = jnp.dot(xb, wu_ref[:, sl].astype(BF16), preferred_element_type=F32)
        h = (_silu(gate) * up).astype(BF16)
        acc = acc + jnp.dot(h, wd_ref[sl, :].astype(BF16), preferred_element_type=F32)
    return alpha * x + 0.5 * acc


def _ffn_ln_kernel(x_ref, wg_ref, wu_ref, wd_ref, g_ref, b_ref, o_ref, *, alpha, fc):
    y = _ffn_residual(x_ref[...], wg_ref, wu_ref, wd_ref, alpha=alpha, fc=fc)
    o_ref[...] = _layer_norm(y, g_ref[...], b_ref[...])


def _ffn_ln(x, wg, wu, wd, g, b, *, alpha, tm=512, fc=256):
    t, d = x.shape
    consts = (wg, wu, wd, g, b)
    return pl.pallas_call(
        functools.partial(_ffn_ln_kernel, alpha=alpha, fc=fc),
        grid=(t // tm,),
        in_specs=[pl.BlockSpec((tm, d), lambda i: (i, 0))] + [_const_spec(w.shape) for w in consts],
        out_specs=pl.BlockSpec((tm, d), lambda i: (i, 0)),
        out_shape=jax.ShapeDtypeStruct((t, d), F32),
        compiler_params=_params(1),
        name="ffn_ln",
    )(x, *consts)


def _ffn_prep_kernel(x_ref, wg_ref, wu_ref, wd_ref, lg_ref, lb_ref, wqkv_ref, wab_ref, wabt_ref, cw_ref,
                     alog_r_ref, dtb_r_ref, alog_c_ref, dtb_c_ref,
                     x1_ref, q_ref, k_ref, kb_ref, qg_ref, kd_ref, rhs_ref, gc_ref, gl_ref, gcr_ref,
                     ext_ref, *, alpha, fc, tl):
    halo = SUBLANES
    kw = GDN_HEADS * GDN_DK
    nc = tl // CHUNK

    @pl.when(pl.program_id(1) == 0)
    def _():
        ext_ref[0:halo, :] = jnp.zeros((halo, ext_ref.shape[1]), F32)

    y = _ffn_residual(x_ref[...], wg_ref, wu_ref, wd_ref, alpha=alpha, fc=fc)
    x1 = _layer_norm(y, lg_ref[...], lb_ref[...])
    x1_ref[...] = x1
    xb = x1.astype(BF16)
    ext_ref[halo:halo + tl, :] = jnp.dot(xb, wqkv_ref[...], preferred_element_type=F32)
    ab = jnp.dot(xb, wab_ref[...], preferred_element_type=F32)
    arow = lax.dot_general(wabt_ref[...], xb, (((1,), (1,)), ((), ())), preferred_element_type=F32)

    row = lax.broadcasted_iota(jnp.int32, (tl, tl), 0)
    col = lax.broadcasted_iota(jnp.int32, (tl, tl), 1)
    same = (row // CHUNK) == (col // CHUNK)
    tri_b = jnp.where(same & (row >= col), 1.0, 0.0).astype(BF16)
    utri_b = jnp.where(same & (row <= col), 1.0, 0.0).astype(BF16)
    g_col = -jnp.exp(alog_r_ref[...]) * _softplus(ab + dtb_r_ref[...])
    gc_col = sum(jnp.dot(tri_b, part, preferred_element_type=F32) for part in _split3(g_col))
    gl_col = jnp.concatenate(
        [jnp.broadcast_to(gc_col[(c + 1) * CHUNK - 1:(c + 1) * CHUNK, :], (CHUNK, LANES))
         for c in range(nc)], axis=0)
    g_row = -jnp.exp(alog_c_ref[...]) * _softplus(arow + dtb_c_ref[...])
    gcr_ref[...] = sum(jnp.dot(part, utri_b, preferred_element_type=F32) for part in _split3(g_row))
    gc_ref[...] = gc_col
    gl_ref[...] = gl_col
    beta_all = jax.nn.sigmoid(ab)
    e_all = jnp.exp(gc_col)
    d_all = jnp.exp(gl_col - gc_col)

    def conv_silu(c0):
        acc = None
        for j in reversed(range(SHORT_CONV)):
            r0 = halo - (SHORT_CONV - 1) + j
            term = cw_ref[j:j + 1, c0:c0 + LANES] * ext_ref[r0:r0 + tl, c0:c0 + LANES]
            acc = term if acc is None else acc + term
        return _silu(acc)

    for h in range(GDN_HEADS):
        hs = slice(h * GDN_DK, (h + 1) * GDN_DK)
        q = conv_silu(h * GDN_DK)
        k = conv_silu(kw + h * GDN_DK)
        v = conv_silu(2 * kw + h * GDN_DV)
        q = q * (lax.rsqrt(jnp.sum(q * q, axis=-1, keepdims=True) + RMS_EPS) * (GDN_DK ** -0.5))
        k = k * lax.rsqrt(jnp.sum(k * k, axis=-1, keepdims=True) + RMS_EPS)
        beta = beta_all[:, GDN_HEADS + h:GDN_HEADS + h + 1]
        e_g = e_all[:, h:h + 1]
        kb = k * beta
        q_ref[:, hs] = q.astype(BF16)
        k_ref[:, hs] = k.astype(BF16)
        kb_ref[:, hs] = kb.astype(BF16)
        qg_ref[:, hs] = (q * e_g).astype(BF16)
        kd_ref[:, hs] = (k * d_all[:, h:h + 1]).astype(BF16)
        rhs_ref[:, 2 * h * GDN_DV:(2 * h + 1) * GDN_DV] = (v * beta).astype(BF16)
        rhs_ref[:, (2 * h + 1) * GDN_DV:(2 * h + 2) * GDN_DV] = (kb * e_g).astype(BF16)

    ext_ref[0:halo, :] = ext_ref[tl:tl + halo, :]


def _ffn_prep(x, wg, wu, wd, lg, lb, wqkv, wab, wabt, cw, alog_r, dtb_r, alog_c, dtb_c,
              *, alpha, batch, tl=256, fc=2816):
    t, d = x.shape
    nl = t // batch // tl
    kw = GDN_HEADS * GDN_DK
    vw = GDN_HEADS * GDN_DV
    nr = wabt.shape[0]
    row = lambda b, l: (b * nl + l, 0)
    consts = (wg, wu, wd, lg, lb, wqkv, wab, wabt, cw, alog_r, dtb_r, alog_c, dtb_c)
    wide = lambda n, dt: (pl.BlockSpec((tl, n), row), jax.ShapeDtypeStruct((t, n), dt))
    outs = [wide(d, F32)] + [wide(kw, BF16)] * 5 + [wide(2 * vw, BF16), wide(LANES, F32), wide(LANES, F32),
                                                    (pl.BlockSpec((nr, tl), lambda b, l: (0, b * nl + l)),
                                                     jax.ShapeDtypeStruct((nr, t), F32))]
    return pl.pallas_call(
        functools.partial(_ffn_prep_kernel, alpha=alpha, fc=fc, tl=tl),
        grid=(batch, nl),
        in_specs=[pl.BlockSpec((tl, d), row)] + [_const_spec(w.shape) for w in consts],
        out_specs=[o[0] for o in outs],
        out_shape=[o[1] for o in outs],
        scratch_shapes=[pltpu.VMEM((SUBLANES + tl, wqkv.shape[1]), F32)],
        compiler_params=_params(2),
        name="ffn_prep",
    )(x, *consts)


def _conv31_block(w_ref, ext_ref, y_ref, c0, t0, *, rb, anchor=None):
    cols = slice(c0, c0 + LANES)
    acc = None
    for r in range(SUBLANES):
        z = None
        for q in range((CONV_WIDTH - 1 - r) // SUBLANES + 1):
            j = CONV_WIDTH - 1 - (SUBLANES * q + r)
            lo = CONV_HALO + t0 - SUBLANES * (q + 1)
            w = w_ref[j:j + 1, cols]
            if anchor is not None:
                w = jnp.where(anchor[0], w, anchor[1])
            term = w * ext_ref[lo:lo + rb + SUBLANES, cols]
            z = term if z is None else z + term
        sh = z[SUBLANES - r:SUBLANES - r + rb]
        acc = sh if acc is None else acc + sh
    y_ref[t0:t0 + rb, cols] = acc


def _mixer_kernel(x_ref, q_ref, k_ref, kb_ref, qg_ref, kd_ref, rhs_ref, gc_ref, gl_ref, gcr_ref,
                  ng_ref, wz_ref, wglu_ref, wgate_ref, cw_ref, cb_ref, lg_ref, lb_ref, ones_ref,
                  o_ref, cact_ref, sg_ref, s_ref, ext_ref, y_ref, *, tl, rb, plain_blocks, anchor_every):
    nc = tl // CHUNK
    heads = range(GDN_HEADS)
    ch = ext_ref.shape[1]

    @pl.when(pl.program_id(1) == 0)
    def _():
        s_ref[...] = jnp.zeros(s_ref.shape, F32)
        ext_ref[0:CONV_HALO, :] = jnp.zeros((CONV_HALO, ch), F32)

    xb = x_ref[...].astype(BF16)
    for c0 in range(0, ch, LANES):
        glu = jnp.dot(xb, wglu_ref[:, 2 * c0:2 * c0 + 2 * LANES], preferred_element_type=F32)
        ext_ref[CONV_HALO:CONV_HALO + tl, c0:c0 + LANES] = glu[:, :LANES] * jax.nn.sigmoid(glu[:, LANES:])
    blocks = [(c0, t0) for c0 in range(0, ch, LANES) for t0 in range(0, tl, rb)]
    late = blocks[plain_blocks:]
    for c0, t0 in blocks[:plain_blocks]:
        _conv31_block(cw_ref, ext_ref, y_ref, c0, t0, rb=rb)
    always = ones_ref[...] > 0.5
    ticks = [0]

    def tick(val):
        ticks[0] += 1
        if late and ticks[0] % anchor_every == 0:
            c0, t0 = late.pop(0)
            _conv31_block(cw_ref, ext_ref, y_ref, c0, t0, rb=rb, anchor=(always, val[0:1, 0:LANES]))
        return val

    zs = _silu(jnp.dot(xb, wz_ref[...], preferred_element_type=F32))
    sg_ref[...] = jax.nn.sigmoid(
        jnp.dot(xb, wgate_ref[...], preferred_element_type=F32)).astype(sg_ref.dtype)

    row = lax.broadcasted_iota(jnp.int32, (tl, tl), 0)
    col = lax.broadcasted_iota(jnp.int32, (tl, tl), 1)
    same_b = jnp.where((row // CHUNK) == (col // CHUNK), 1.0, 0.0).astype(BF16)
    pi = lax.broadcasted_iota(jnp.int32, (CHUNK, tl), 0)
    lane = lax.broadcasted_iota(jnp.int32, (CHUNK, tl), 1)
    lc = lane // CHUNK
    lj = lane % CHUNK
    tri_p = pi >= lj
    strict_p = pi > lj
    eye_p = jnp.where(pi == lj, 1.0, 0.0)
    base = 16
    base_p = (pi // base) == (lj // base)

    def pack(full):
        p = full[0:CHUNK]
        for c in range(1, nc):
            p = jnp.where(lc == c, full[c * CHUNK:(c + 1) * CHUNK], p)
        return p

    def blockdiag(p):
        return jnp.concatenate([p.astype(BF16)] * nc, axis=0) * same_b

    def pmm(a, b):
        return jnp.dot(a.astype(BF16), blockdiag(b), preferred_element_type=F32)

    m, a = [], []
    for h in heads:
        hs = slice(h * GDN_DK, (h + 1) * GDN_DK)
        k = k_ref[:, hs]
        g_pack = pack(jnp.broadcast_to(gc_ref[:, h:h + 1], (tl, tl)))
        diff = g_pack - gcr_ref[h:h + 1, :]
        decay = jnp.where(tri_p, jnp.exp(jnp.where(tri_p, diff, 0.0)), 0.0)
        m.append(jnp.where(strict_p, pack(_mm_nt(kb_ref[:, hs], k)) * decay, 0.0))
        a.append(tick(pack(_mm_nt(q_ref[:, hs], k)) * decay))

    n = [jnp.where(base_p, -m[h], 0.0) for h in heads]
    inv = [eye_p + n[h] for h in heads]
    for _ in range(3):
        n = [tick(pmm(n[h], n[h])) for h in heads]
        inv = [tick(inv[h] + pmm(n[h], inv[h])) for h in heads]
    s = base
    while s < CHUNK:
        lower_left = ((pi // (2 * s)) == (lj // (2 * s))) & ((pi // s) == (lj // s) + 1)
        e = [tick(pmm(jnp.where(lower_left, m[h], 0.0), inv[h])) for h in heads]
        inv = [tick(inv[h] - pmm(inv[h], e[h])) for h in heads]
        s *= 2
    xb16 = [jnp.dot(blockdiag(inv[h]), rhs_ref[:, 2 * h * GDN_DV:(2 * h + 2) * GDN_DV],
                    preferred_element_type=F32).astype(BF16) for h in heads]

    ax = [tick(jnp.dot(blockdiag(a[h]), xb16[h], preferred_element_type=F32)) for h in heads]
    kx = [[lax.dot_general(kd_ref[c * CHUNK:(c + 1) * CHUNK, h * GDN_DK:(h + 1) * GDN_DK],
                           xb16[h][c * CHUNK:(c + 1) * CHUNK], (((0,), (0,)), ((), ())),
                           preferred_element_type=F32) for c in range(nc)] for h in heads]
    state = [s_ref[h] for h in heads]
    outs = [[] for _ in heads]
    for c in range(nc):
        r = slice(c * CHUNK, (c + 1) * CHUNK)
        for h in heads:
            hs = slice(h * GDN_DK, (h + 1) * GDN_DK)
            sb = state[h].astype(BF16)
            r_c = (qg_ref[r, hs].astype(F32) - ax[h][r, GDN_DV:]).astype(BF16)
            outs[h].append(jnp.dot(r_c, sb, preferred_element_type=F32) + ax[h][r, :GDN_DV])
            state[h] = (state[h] * jnp.exp(gl_ref[c * CHUNK:c * CHUNK + 1, h:h + 1])
                        - jnp.dot(kx[h][c][:, GDN_DV:].astype(BF16), sb, preferred_element_type=F32)
                        + kx[h][c][:, :GDN_DV])
            tick(state[h])

    for c0, t0 in late:
        _conv31_block(cw_ref, ext_ref, y_ref, c0, t0, rb=rb)
    cact_ref[...] = _silu(_layer_norm(y_ref[...] + cb_ref[...], lg_ref[...], lb_ref[...])
                          ).astype(cact_ref.dtype)
    ext_ref[0:CONV_HALO, :] = ext_ref[tl:tl + CONV_HALO, :]
    for h in heads:
        hs = slice(h * GDN_DV, (h + 1) * GDN_DV)
        s_ref[h] = state[h]
        o = jnp.concatenate(outs[h], axis=0)
        o = o * lax.rsqrt(jnp.mean(o * o, axis=-1, keepdims=True) + RMS_EPS) * ng_ref[...]
        o_ref[:, hs] = (o * zs[:, hs]).astype(o_ref.dtype)


def _mixer(x, q, k, kb, qg, kd, rhs, gc, gl, gcr, ng, wz, wglu, wgate, cw, cb, lg, lb,
           *, batch, tl=256, rb=64, plain_blocks=20, anchor_every=8):
    t = x.shape[0]
    nl = t // batch // tl
    vw, ch, ns = wz.shape[1], wglu.shape[1] // 2, wgate.shape[1]
    row = lambda b, l: (b * nl + l, 0)
    tiles = (x, q, k, kb, qg, kd, rhs, gc, gl)
    consts = (ng, wz, wglu, wgate, cw, cb, lg, lb, jnp.ones((1, LANES), F32))
    return pl.pallas_call(
        functools.partial(_mixer_kernel, tl=tl, rb=rb, plain_blocks=plain_blocks,
                          anchor_every=anchor_every),
        grid=(batch, nl),
        in_specs=([pl.BlockSpec((tl, v.shape[1]), row) for v in tiles]
                  + [pl.BlockSpec((gcr.shape[0], tl), lambda b, l: (0, b * nl + l))]
                  + [_const_spec(w.shape) for w in consts]),
        out_specs=[pl.BlockSpec((tl, vw), row), pl.BlockSpec((tl, ch), row), pl.BlockSpec((tl, ns), row)],
        out_shape=[jax.ShapeDtypeStruct((t, vw), BF16), jax.ShapeDtypeStruct((t, ch), BF16),
                   jax.ShapeDtypeStruct((t, ns), BF16)],
        scratch_shapes=[pltpu.VMEM((GDN_HEADS, GDN_DK, GDN_DV), F32),
                        pltpu.VMEM((CONV_HALO + tl, ch), F32), pltpu.VMEM((tl, ch), F32)],
        compiler_params=_params(2),
        name="mixer",
    )(*tiles, gcr, *consts)


def _merge_kernel(x_ref, o_ref, c_ref, sg_ref, wa_ref, wc_ref, bc_ref, wm_ref, g_ref, b_ref,
                  out_ref, *, alpha):
    d = x_ref.shape[1]
    y_a = jnp.dot(o_ref[...], wa_ref[...].astype(BF16), preferred_element_type=F32)
    y_c = jnp.dot(c_ref[...], wc_ref[...].astype(BF16), preferred_element_type=F32) + bc_ref[...]
    y = sg_ref[:, :d] * y_a + sg_ref[:, d:] * y_c
    mixed = jnp.dot(y.astype(BF16), wm_ref[...].astype(BF16), preferred_element_type=F32)
    out_ref[...] = _layer_norm(alpha * x_ref[...] + mixed, g_ref[...], b_ref[...])


def _merge(x, o, c, sg, wa, wc, bc, wm, g, b, *, alpha, tm=1024):
    t, d = x.shape
    row = lambda i: (i, 0)
    return pl.pallas_call(
        functools.partial(_merge_kernel, alpha=alpha),
        grid=(t // tm,),
        in_specs=[pl.BlockSpec((tm, d), row), pl.BlockSpec((tm, o.shape[1]), row),
                  pl.BlockSpec((tm, c.shape[1]), row), pl.BlockSpec((tm, sg.shape[1]), row),
                  _const_spec(wa.shape), _const_spec(wc.shape), _const_spec(bc.shape),
                  _const_spec(wm.shape), _const_spec(g.shape), _const_spec(b.shape)],
        out_specs=pl.BlockSpec((tm, d), row),
        out_shape=jax.ShapeDtypeStruct((t, d), F32),
        compiler_params=_params(1),
        name="merge",
    )(x, o, c, sg, wa, wc, bc, wm, g, b)


def _kv_proj_kernel(m_ref, w_ref, o_ref):
    o_ref[...] = jnp.dot(m_ref[...].astype(BF16), w_ref[...].astype(BF16),
                         preferred_element_type=F32).astype(o_ref.dtype)


def _kv_proj(mem, w, *, tm=256):
    t, d = mem.shape
    n = w.shape[1]
    return pl.pallas_call(
        _kv_proj_kernel,
        grid=(t // tm,),
        in_specs=[pl.BlockSpec((tm, d), lambda i: (i, 0)), _const_spec(w.shape)],
        out_specs=pl.BlockSpec((tm, n), lambda i: (i, 0)),
        out_shape=jax.ShapeDtypeStruct((t, n), BF16),
        compiler_params=_params(1),
        name="kv_proj",
    )(mem, w)


def _xattn_kernel(x_ref, kv_ref, wq_ref, wo_ref, g_ref, b_ref, out_ref, *, alpha):
    x = x_ref[...]
    d = x.shape[1]
    hd = d // XATTN_HEADS
    q = jnp.dot(x.astype(BF16), wq_ref[...].astype(BF16), preferred_element_type=F32).astype(BF16)
    heads = []
    for h in range(XATTN_HEADS):
        k = kv_ref[:, h * hd:(h + 1) * hd]
        v = kv_ref[:, d + h * hd:d + (h + 1) * hd]
        s = _mm_nt(q[:, h * hd:(h + 1) * hd], k) * (hd ** -0.5)
        e = jnp.exp(s - jnp.max(s, axis=-1, keepdims=True))
        p = e / jnp.sum(e, axis=-1, keepdims=True)
        heads.append(jnp.dot(p.astype(BF16), v, preferred_element_type=F32).astype(BF16))
    o = jnp.concatenate(heads, axis=1)
    att = jnp.dot(o, wo_ref[...].astype(BF16), preferred_element_type=F32)
    out_ref[...] = _layer_norm(alpha * x + att, g_ref[...], b_ref[...])


def _xattn(x, kv, wq, wo, g, b, *, alpha, batch, tl=1024):
    t, d = x.shape
    nl = t // batch // tl
    n_mem = kv.shape[0] // batch
    row = lambda bi, l: (bi * nl + l, 0)
    return pl.pallas_call(
        functools.partial(_xattn_kernel, alpha=alpha),
        grid=(batch, nl),
        in_specs=[pl.BlockSpec((tl, d), row),
                  pl.BlockSpec((n_mem, kv.shape[1]), lambda bi, l: (bi, 0)),
                  _const_spec(wq.shape), _const_spec(wo.shape),
                  _const_spec(g.shape), _const_spec(b.shape)],
        out_specs=pl.BlockSpec((tl, d), row),
        out_shape=jax.ShapeDtypeStruct((t, d), F32),
        compiler_params=_params(2),
        name="xattn",
    )(x, kv, wq, wo, g, b)


def _row(v):
    return v.reshape(1, -1).astype(F32)


def _pad_to(v, shape):
    return jnp.pad(v, tuple((0, n - s) for s, n in zip(v.shape, shape)))


def _interleave_halves(w):
    k, n = w.shape
    return w.reshape(k, 2, n // 2 // LANES, LANES).transpose(0, 2, 1, 3).reshape(k, n)


def _layer(x, mem, p, *, batch, alpha):
    d = x.shape[1]
    kw = GDN_HEADS * GDN_DK
    vw = GDN_HEADS * GDN_DV
    w_in = p["w_in"]
    o0 = 2 * kw + vw
    o1 = o0 + vw
    o2 = o1 + 2 * GDN_HEADS
    o3 = o2 + 2 * d
    w_ab = w_in[:, o1:o2]
    wab = _pad_to(w_ab, (d, LANES)).astype(BF16)
    wabt = _pad_to(w_ab[:, :GDN_HEADS].T, (BF16_ROWS, d)).astype(BF16)
    alog = _row(p["gdn_a_log"])
    dtb = _row(p["gdn_dt_bias"])

    x, q, k, kb, qg, kd, rhs, gc, gl, gcr = _ffn_prep(
        x, p["ffn1_wg"].astype(BF16), p["ffn1_wu"].astype(BF16), p["ffn1_wd"].astype(BF16),
        _row(p["ln1_g"]), _row(p["ln1_b"]), w_in[:, :o0].astype(BF16), wab, wabt,
        p["gdn_conv_qkv"].astype(F32), _pad_to(alog, (1, LANES)), _pad_to(dtb, (1, LANES)),
        _pad_to(alog.reshape(-1, 1), (BF16_ROWS, 1)), _pad_to(dtb.reshape(-1, 1), (BF16_ROWS, 1)),
        alpha=alpha, batch=batch)
    o, c, sg = _mixer(
        x, q, k, kb, qg, kd, rhs, gc, gl, gcr, _row(p["gdn_norm_g"]),
        w_in[:, o0:o1].astype(BF16), _interleave_halves(w_in[:, o2:o3]).astype(BF16),
        w_in[:, o3:].astype(BF16),
        p["conv_dw_w"].astype(F32), _row(p["conv_dw_b"]), _row(p["conv_ln_g"]), _row(p["conv_ln_b"]),
        batch=batch)
    x = _merge(x, o, c, sg, p["w_gdn_out"], p["w_conv_out"], _row(p["b_conv_out"]), p["w_mix_out"],
               _row(p["ln2_g"]), _row(p["ln2_b"]), alpha=alpha)

    kv = _kv_proj(mem, p["w_xkv"])
    x = _xattn(x, kv, p["w_xq"], p["w_xo"], _row(p["ln3_g"]), _row(p["ln3_b"]), alpha=alpha, batch=batch)

    return _ffn_ln(x, p["ffn2_wg"], p["ffn2_wu"], p["ffn2_wd"],
                   _row(p["ln4_g"]), _row(p["ln4_b"]), alpha=alpha)


def kernel(x, mem, ffn1_wg, ffn1_wu, ffn1_wd, ln1_g, ln1_b, w_in, gdn_conv_qkv, gdn_a_log, gdn_dt_bias, gdn_norm_g, w_gdn_out, conv_dw_w, conv_dw_b, conv_ln_g, conv_ln_b, w_conv_out, b_conv_out, w_mix_out, ln2_g, ln2_b, w_xq, w_xkv, w_xo, ln3_g, ln3_b, ffn2_wg, ffn2_wu, ffn2_wd, ln4_g, ln4_b):
    weights = dict(
        ffn1_wg=ffn1_wg, ffn1_wu=ffn1_wu, ffn1_wd=ffn1_wd, ln1_g=ln1_g, ln1_b=ln1_b, w_in=w_in,
        gdn_conv_qkv=gdn_conv_qkv, gdn_a_log=gdn_a_log, gdn_dt_bias=gdn_dt_bias,
        gdn_norm_g=gdn_norm_g, w_gdn_out=w_gdn_out, conv_dw_w=conv_dw_w, conv_dw_b=conv_dw_b,
        conv_ln_g=conv_ln_g, conv_ln_b=conv_ln_b, w_conv_out=w_conv_out, b_conv_out=b_conv_out,
        w_mix_out=w_mix_out, ln2_g=ln2_g, ln2_b=ln2_b, w_xq=w_xq, w_xkv=w_xkv, w_xo=w_xo,
        ln3_g=ln3_g, ln3_b=ln3_b, ffn2_wg=ffn2_wg, ffn2_wu=ffn2_wu, ffn2_wd=ffn2_wd,
        ln4_g=ln4_g, ln4_b=ln4_b)
    batch, seq, d = x.shape
    depth = ffn1_wg.shape[0]
    alpha = (2 * depth) ** 0.25
    h = x.reshape(batch * seq, d)
    m = mem.reshape(-1, d)
    for i in range(depth):
        h = _layer(h, m, {k: v[i] for k, v in weights.items()}, batch=batch, alpha=alpha)
    return h.reshape(batch, seq, d)
```

```python
import functools
from typing import NamedTuple

import jax
import jax.numpy as jnp
from jax import lax
from jax.experimental import pallas as pl
from jax.experimental.pallas import tpu as pltpu

F32 = jnp.float32
BF16 = jnp.bfloat16

LN_EPS = 1e-5
RMS_EPS = 1e-6
GDN_HEADS = 8
GDN_DK = 128
GDN_DV = 128
SHORT_CONV = 4
CHUNK = 64
CONV_WIDTH = 31
XATTN_HEADS = 4
N_MEM = 256

LANES = 128
SUBLANES = 8
BF16_ROWS = 16
VMEM_LIMIT = 56 * 1024 * 1024
CONV_HALO = 5 * SUBLANES


def _params(n_axes):
    return pltpu.CompilerParams(dimension_semantics=("arbitrary",) * n_axes,
                                vmem_limit_bytes=VMEM_LIMIT)


def _const_spec(shape):
    zeros = (0,) * len(shape)
    return pl.BlockSpec(shape, lambda *_: zeros, pipeline_mode=pl.Buffered(1))


class _Cols(NamedTuple):
    array: jax.Array
    width: int
    block: int

    @property
    def shape(self):
        return (self.array.shape[0], self.width)


def _weight_spec(w):
    if isinstance(w, _Cols):
        return pl.BlockSpec(w.shape, lambda *_, b=w.block: (0, b), pipeline_mode=pl.Buffered(1))
    return _const_spec(w.shape)


def _weight_operand(w):
    return w.array if isinstance(w, _Cols) else w


def _mm_nt(a, b):
    return lax.dot_general(a.astype(BF16), b.astype(BF16), (((1,), (1,)), ((), ())),
                           preferred_element_type=F32)


def _split3(v):
    h1 = v.astype(BF16)
    r1 = v - h1.astype(F32)
    h2 = r1.astype(BF16)
    h3 = (r1 - h2.astype(F32)).astype(BF16)
    return h1, h2, h3


def _layer_norm(v, g, b):
    mu = jnp.mean(v, axis=-1, keepdims=True)
    d = v - mu
    var = jnp.mean(d * d, axis=-1, keepdims=True)
    return d * lax.rsqrt(var + LN_EPS) * g + b


def _silu(v):
    return v * jax.nn.sigmoid(v)


def _softplus(v):
    return jnp.maximum(v, 0.0) + jnp.log1p(jnp.exp(-jnp.abs(v)))


def _ffn_residual(x, wg_ref, wu_ref, wd_ref, *, alpha, fc):
    xb = x.astype(BF16)
    acc = jnp.zeros(x.shape, F32)
    for j in range(wg_ref.shape[1] // fc):
        sl = slice(j * fc, (j + 1) * fc)
        gate = jnp.dot(xb, wg_ref[:, sl].astype(BF16), preferred_element_type=F32)
        up = jnp.dot(xb, wu_ref[:, sl].astype(BF16), preferred_element_type=F32)
        h = (_silu(gate) * up).astype(BF16)
        acc = acc + jnp.dot(h, wd_ref[sl, :].astype(BF16), preferred_element_type=F32)
    return alpha * x + 0.5 * acc


def _ffn_ln_kernel(x_ref, wg_ref, wu_ref, wd_ref, g_ref, b_ref, o_ref, *, alpha, fc):
    y = _ffn_residual(x_ref[...], wg_ref, wu_ref, wd_ref, alpha=alpha, fc=fc)
    o_ref[...] = _layer_norm(y, g_ref[...], b_ref[...])


def _ffn_ln(x, wg, wu, wd, g, b, *, alpha, tm=512, fc=256):
    t, d = x.shape
    consts = (wg, wu, wd, g, b)
    return pl.pallas_call(
        functools.partial(_ffn_ln_kernel, alpha=alpha, fc=fc),
        grid=(t // tm,),
        in_specs=[pl.BlockSpec((tm, d), lambda i: (i, 0))] + [_const_spec(w.shape) for w in consts],
        out_specs=pl.BlockSpec((tm, d), lambda i: (i, 0)),
        out_shape=jax.ShapeDtypeStruct((t, d), F32),
        compiler_params=_params(1),
        name="ffn_ln",
    )(x, *consts)


def _ffn_prep_kernel(x_ref, wg_ref, wu_ref, wd_ref, lg_ref, lb_ref, wqkv_ref, wab_ref, wabt_ref, cw_ref,
                     alog_r_ref, dtb_r_ref, alog_c_ref, dtb_c_ref,
                     x1_ref, q_ref, k_ref, kb_ref, qg_ref, kd_ref, rhs_ref, gc_ref, gl_ref, gcr_ref,
                     ext_ref, *, alpha, fc, tl):
    halo = SUBLANES
    kw = GDN_HEADS * GDN_DK
    nc = tl // CHUNK

    @pl.when(pl.program_id(1) == 0)
    def _():
        ext_ref[0:halo, :] = jnp.zeros((halo, ext_ref.shape[1]), F32)

    y = _ffn_residual(x_ref[...], wg_ref, wu_ref, wd_ref, alpha=alpha, fc=fc)
    x1 = _layer_norm(y, lg_ref[...], lb_ref[...])
    x1_ref[...] = x1
    xb = x1.astype(BF16)
    ext_ref[halo:halo + tl, :] = jnp.dot(xb, wqkv_ref[...], preferred_element_type=F32)
    ab = jnp.dot(xb, wab_ref[...], preferred_element_type=F32)
    arow = lax.dot_general(wabt_ref[...], xb, (((1,), (1,)), ((), ())), preferred_element_type=F32)

    row = lax.broadcasted_iota(jnp.int32, (tl, tl), 0)
    col = lax.broadcasted_iota(jnp.int32, (tl, tl), 1)
    same = (row // CHUNK) == (col // CHUNK)
    tri_b = jnp.where(same & (row >= col), 1.0, 0.0).astype(BF16)
    utri_b = jnp.where(same & (row <= col), 1.0, 0.0).astype(BF16)
    g_col = -jnp.exp(alog_r_ref[...]) * _softplus(ab + dtb_r_ref[...])
    gc_col = sum(jnp.dot(tri_b, part, preferred_element_type=F32) for part in _split3(g_col))
    gl_col = jnp.concatenate(
        [jnp.broadcast_to(gc_col[(c + 1) * CHUNK - 1:(c + 1) * CHUNK, :], (CHUNK, LANES))
         for c in range(nc)], axis=0)
    g_row = -jnp.exp(alog_c_ref[...]) * _softplus(arow + dtb_c_ref[...])
    gcr_ref[...] = sum(jnp.dot(part, utri_b, preferred_element_type=F32) for part in _split3(g_row))
    gc_ref[...] = gc_col
    gl_ref[...] = gl_col
    beta_all = jax.nn.sigmoid(ab)
    e_all = jnp.exp(gc_col)
    d_all = jnp.exp(gl_col - gc_col)

    def conv_silu(c0):
        acc = None
        for j in reversed(range(SHORT_CONV)):
            r0 = halo - (SHORT_CONV - 1) + j
            term = cw_ref[j:j + 1, c0:c0 + LANES] * ext_ref[r0:r0 + tl, c0:c0 + LANES]
            acc = term if acc is None else acc + term
        return _silu(acc)

    for h in range(GDN_HEADS):
        hs = slice(h * GDN_DK, (h + 1) * GDN_DK)
        q = conv_silu(h * GDN_DK)
        k = conv_silu(kw + h * GDN_DK)
        v = conv_silu(2 * kw + h * GDN_DV)
        q = q * (lax.rsqrt(jnp.sum(q * q, axis=-1, keepdims=True) + RMS_EPS) * (GDN_DK ** -0.5))
        k = k * lax.rsqrt(jnp.sum(k * k, axis=-1, keepdims=True) + RMS_EPS)
        beta = beta_all[:, GDN_HEADS + h:GDN_HEADS + h + 1]
        e_g = e_all[:, h:h + 1]
        kb = k * beta
        q_ref[:, hs] = q.astype(BF16)
        k_ref[:, hs] = k.astype(BF16)
        kb_ref[:, hs] = kb.astype(BF16)
        qg_ref[:, hs] = (q * e_g).astype(BF16)
        kd_ref[:, hs] = (k * d_all[:, h:h + 1]).astype(BF16)
        rhs_ref[:, 2 * h * GDN_DV:(2 * h + 1) * GDN_DV] = (v * beta).astype(BF16)
        rhs_ref[:, (2 * h + 1) * GDN_DV:(2 * h + 2) * GDN_DV] = (kb * e_g).astype(BF16)

    ext_ref[0:halo, :] = ext_ref[tl:tl + halo, :]


def _ffn_prep(x, wg, wu, wd, lg, lb, wqkv, wab, wabt, cw, alog_r, dtb_r, alog_c, dtb_c,
              *, alpha, batch, tl=256, fc=2816):
    t, d = x.shape
    nl = t // batch // tl
    kw = GDN_HEADS * GDN_DK
    vw = GDN_HEADS * GDN_DV
    nr = wabt.shape[0]
    row = lambda b, l: (b * nl + l, 0)
    consts = (wg, wu, wd, lg, lb, wqkv, wab, wabt, cw, alog_r, dtb_r, alog_c, dtb_c)
    wide = lambda n, dt: (pl.BlockSpec((tl, n), row), jax.ShapeDtypeStruct((t, n), dt))
    outs = [wide(d, F32)] + [wide(kw, BF16)] * 5 + [wide(2 * vw, BF16), wide(LANES, F32), wide(LANES, F32),
                                                    (pl.BlockSpec((nr, tl), lambda b, l: (0, b * nl + l)),
                                                     jax.ShapeDtypeStruct((nr, t), F32))]
    return pl.pallas_call(
        functools.partial(_ffn_prep_kernel, alpha=alpha, fc=fc, tl=tl),
        grid=(batch, nl),
        in_specs=[pl.BlockSpec((tl, d), row)] + [_weight_spec(w) for w in consts],
        out_specs=[o[0] for o in outs],
        out_shape=[o[1] for o in outs],
        scratch_shapes=[pltpu.VMEM((SUBLANES + tl, wqkv.shape[1]), F32)],
        compiler_params=_params(2),
        name="ffn_prep",
    )(x, *[_weight_operand(w) for w in consts])


def _conv31_block(w_ref, ext_ref, y_ref, c0, t0, *, rb, anchor=None):
    cols = slice(c0, c0 + LANES)
    acc = None
    for r in range(SUBLANES):
        z = None
        for q in range((CONV_WIDTH - 1 - r) // SUBLANES + 1):
            j = CONV_WIDTH - 1 - (SUBLANES * q + r)
            lo = CONV_HALO + t0 - SUBLANES * (q + 1)
            w = w_ref[j:j + 1, cols]
            if anchor is not None:
                w = jnp.where(anchor[0], w, anchor[1])
            term = w * ext_ref[lo:lo + rb + SUBLANES, cols]
            z = term if z is None else z + term
        sh = z[SUBLANES - r:SUBLANES - r + rb]
        acc = sh if acc is None else acc + sh
    y_ref[t0:t0 + rb, cols] = acc


def _mixer_kernel(x_ref, q_ref, k_ref, kb_ref, qg_ref, kd_ref, rhs_ref, gc_ref, gl_ref, gcr_ref,
                  ng_ref, wz_ref, wglu_ref, wgate_ref, cw_ref, cb_ref, lg_ref, lb_ref, ones_ref,
                  o_ref, cact_ref, sg_ref, s_ref, ext_ref, y_ref, *, tl, rb, plain_blocks, anchor_every):
    nc = tl // CHUNK
    heads = range(GDN_HEADS)
    ch = ext_ref.shape[1]

    @pl.when(pl.program_id(1) == 0)
    def _():
        s_ref[...] = jnp.zeros(s_ref.shape, F32)
        ext_ref[0:CONV_HALO, :] = jnp.zeros((CONV_HALO, ch), F32)

    xb = x_ref[...].astype(BF16)
    for c0 in range(0, ch, LANES):
        glu = jnp.dot(xb, wglu_ref[:, 2 * c0:2 * c0 + 2 * LANES], preferred_element_type=F32)
        ext_ref[CONV_HALO:CONV_HALO + tl, c0:c0 + LANES] = glu[:, :LANES] * jax.nn.sigmoid(glu[:, LANES:])
    blocks = [(c0, t0) for c0 in range(0, ch, LANES) for t0 in range(0, tl, rb)]
    late = blocks[plain_blocks:]
    for c0, t0 in blocks[:plain_blocks]:
        _conv31_block(cw_ref, ext_ref, y_ref, c0, t0, rb=rb)
    always = ones_ref[...] > 0.5
    ticks = [0]

    def tick(val):
        ticks[0] += 1
        if late and ticks[0] % anchor_every == 0:
            c0, t0 = late.pop(0)
            _conv31_block(cw_ref, ext_ref, y_ref, c0, t0, rb=rb, anchor=(always, val[0:1, 0:LANES]))
        return val

    zs = _silu(jnp.dot(xb, wz_ref[...], preferred_element_type=F32))
    sg_ref[...] = jax.nn.sigmoid(
        jnp.dot(xb, wgate_ref[...], preferred_element_type=F32)).astype(sg_ref.dtype)

    row = lax.broadcasted_iota(jnp.int32, (tl, tl), 0)
    col = lax.broadcasted_iota(jnp.int32, (tl, tl), 1)
    same_b = jnp.where((row // CHUNK) == (col // CHUNK), 1.0, 0.0).astype(BF16)
    pi = lax.broadcasted_iota(jnp.int32, (CHUNK, tl), 0)
    lane = lax.broadcasted_iota(jnp.int32, (CHUNK, tl), 1)
    lc = lane // CHUNK
    lj = lane % CHUNK
    tri_p = pi >= lj
    strict_p = pi > lj
    eye_p = jnp.where(pi == lj, 1.0, 0.0)
    base = 16
    base_p = (pi // base) == (lj // base)

    def pack(full):
        p = full[0:CHUNK]
        for c in range(1, nc):
            p = jnp.where(lc == c, full[c * CHUNK:(c + 1) * CHUNK], p)
        return p

    def blockdiag(p):
        return jnp.concatenate([p.astype(BF16)] * nc, axis=0) * same_b

    def pmm(a, b):
        return jnp.dot(a.astype(BF16), blockdiag(b), preferred_element_type=F32)

    m, a = [], []
    for h in heads:
        hs = slice(h * GDN_DK, (h + 1) * GDN_DK)
        k = k_ref[:, hs]
        g_pack = pack(jnp.broadcast_to(gc_ref[:, h:h + 1], (tl, tl)))
        diff = g_pack - gcr_ref[h:h + 1, :]
        decay = jnp.where(tri_p, jnp.exp(jnp.where(tri_p, diff, 0.0)), 0.0)
        m.append(jnp.where(strict_p, pack(_mm_nt(kb_ref[:, hs], k)) * decay, 0.0))
        a.append(tick(pack(_mm_nt(q_ref[:, hs], k)) * decay))

    n = [jnp.where(base_p, -m[h], 0.0) for h in heads]
    inv = [eye_p + n[h] for h in heads]
    for _ in range(3):
        n = [tick(pmm(n[h], n[h])) for h in heads]
        inv = [tick(inv[h] + pmm(n[h], inv[h])) for h in heads]
    s = base
    while s < CHUNK:
        lower_left = ((pi // (2 * s)) == (lj // (2 * s))) & ((pi // s) == (lj // s) + 1)
        e = [tick(pmm(jnp.where(lower_left, m[h], 0.0), inv[h])) for h in heads]
        inv = [tick(inv[h] - pmm(inv[h], e[h])) for h in heads]
        s *= 2
    xb16 = [jnp.dot(blockdiag(inv[h]), rhs_ref[:, 2 * h * GDN_DV:(2 * h + 2) * GDN_DV],
                    preferred_element_type=F32).astype(BF16) for h in heads]

    ax = [tick(jnp.dot(blockdiag(a[h]), xb16[h], preferred_element_type=F32)) for h in heads]
    kx = [[lax.dot_general(kd_ref[c * CHUNK:(c + 1) * CHUNK, h * GDN_DK:(h + 1) * GDN_DK],
                           xb16[h][c * CHUNK:(c + 1) * CHUNK], (((0,), (0,)), ((), ())),
                           preferred_element_type=F32) for c in range(nc)] for h in heads]
    state = [s_ref[h] for h in heads]
    outs = [[] for _ in heads]
    for c in range(nc):
        r = slice(c * CHUNK, (c + 1) * CHUNK)
        for h in heads:
            hs = slice(h * GDN_DK, (h + 1) * GDN_DK)
            sb = state[h].astype(BF16)
            r_c = (qg_ref[r, hs].astype(F32) - ax[h][r, GDN_DV:]).astype(BF16)
            outs[h].append(jnp.dot(r_c, sb, preferred_element_type=F32) + ax[h][r, :GDN_DV])
            state[h] = (state[h] * jnp.exp(gl_ref[c * CHUNK:c * CHUNK + 1, h:h + 1])
                        - jnp.dot(kx[h][c][:, GDN_DV:].astype(BF16), sb, preferred_element_type=F32)
                        + kx[h][c][:, :GDN_DV])
            tick(state[h])

    for c0, t0 in late:
        _conv31_block(cw_ref, ext_ref, y_ref, c0, t0, rb=rb)
    cact_ref[...] = _silu(_layer_norm(y_ref[...] + cb_ref[...], lg_ref[...], lb_ref[...])
                          ).astype(cact_ref.dtype)
    ext_ref[0:CONV_HALO, :] = ext_ref[tl:tl + CONV_HALO, :]
    for h in heads:
        hs = slice(h * GDN_DV, (h + 1) * GDN_DV)
        s_ref[h] = state[h]
        o = jnp.concatenate(outs[h], axis=0)
        o = o * lax.rsqrt(jnp.mean(o * o, axis=-1, keepdims=True) + RMS_EPS) * ng_ref[...]
        o_ref[:, hs] = (o * zs[:, hs]).astype(o_ref.dtype)


def _mixer(x, q, k, kb, qg, kd, rhs, gc, gl, gcr, ng, wz, wglu, wgate, cw, cb, lg, lb,
           *, batch, tl=256, rb=64, plain_blocks=20, anchor_every=8):
    t = x.shape[0]
    nl = t // batch // tl
    vw, ch, ns = wz.shape[1], wglu.shape[1] // 2, wgate.shape[1]
    row = lambda b, l: (b * nl + l, 0)
    tiles = (x, q, k, kb, qg, kd, rhs, gc, gl)
    consts = (ng, wz, wglu, wgate, cw, cb, lg, lb, jnp.ones((1, LANES), F32))
    return pl.pallas_call(
        functools.partial(_mixer_kernel, tl=tl, rb=rb, plain_blocks=plain_blocks,
                          anchor_every=anchor_every),
        grid=(batch, nl),
        in_specs=([pl.BlockSpec((tl, v.shape[1]), row) for v in tiles]
                  + [pl.BlockSpec((gcr.shape[0], tl), lambda b, l: (0, b * nl + l))]
                  + [_weight_spec(w) for w in consts]),
        out_specs=[pl.BlockSpec((tl, vw), row), pl.BlockSpec((tl, ch), row), pl.BlockSpec((tl, ns), row)],
        out_shape=[jax.ShapeDtypeStruct((t, vw), BF16), jax.ShapeDtypeStruct((t, ch), BF16),
                   jax.ShapeDtypeStruct((t, ns), BF16)],
        scratch_shapes=[pltpu.VMEM((GDN_HEADS, GDN_DK, GDN_DV), F32),
                        pltpu.VMEM((CONV_HALO + tl, ch), F32), pltpu.VMEM((tl, ch), F32)],
        compiler_params=_params(2),
        name="mixer",
    )(*tiles, gcr, *[_weight_operand(w) for w in consts])


def _merge_kernel(x_ref, o_ref, c_ref, sg_ref, wa_ref, wc_ref, bc_ref, wm_ref, g_ref, b_ref,
                  out_ref, *, alpha):
    d = x_ref.shape[1]
    y_a = jnp.dot(o_ref[...], wa_ref[...].astype(BF16), preferred_element_type=F32)
    y_c = jnp.dot(c_ref[...], wc_ref[...].astype(BF16), preferred_element_type=F32) + bc_ref[...]
    y = sg_ref[:, :d] * y_a + sg_ref[:, d:] * y_c
    mixed = jnp.dot(y.astype(BF16), wm_ref[...].astype(BF16), preferred_element_type=F32)
    out_ref[...] = _layer_norm(alpha * x_ref[...] + mixed, g_ref[...], b_ref[...])


def _merge(x, o, c, sg, wa, wc, bc, wm, g, b, *, alpha, tm=1024):
    t, d = x.shape
    row = lambda i: (i, 0)
    return pl.pallas_call(
        functools.partial(_merge_kernel, alpha=alpha),
        grid=(t // tm,),
        in_specs=[pl.BlockSpec((tm, d), row), pl.BlockSpec((tm, o.shape[1]), row),
                  pl.BlockSpec((tm, c.shape[1]), row), pl.BlockSpec((tm, sg.shape[1]), row),
                  _const_spec(wa.shape), _const_spec(wc.shape), _const_spec(bc.shape),
                  _const_spec(wm.shape), _const_spec(g.shape), _const_spec(b.shape)],
        out_specs=pl.BlockSpec((tm, d), row),
        out_shape=jax.ShapeDtypeStruct((t, d), F32),
        compiler_params=_params(1),
        name="merge",
    )(x, o, c, sg, wa, wc, bc, wm, g, b)


def _kv_proj_kernel(m_ref, w_ref, o_ref):
    o_ref[...] = jnp.dot(m_ref[...].astype(BF16), w_ref[...].astype(BF16),
                         preferred_element_type=F32).astype(o_ref.dtype)


def _kv_proj(mem, w, *, tm=256):
    t, d = mem.shape
    n = w.shape[1]
    return pl.pallas_call(
        _kv_proj_kernel,
        grid=(t // tm,),
        in_specs=[pl.BlockSpec((tm, d), lambda i: (i, 0)), _const_spec(w.shape)],
        out_specs=pl.BlockSpec((tm, n), lambda i: (i, 0)),
        out_shape=jax.ShapeDtypeStruct((t, n), BF16),
        compiler_params=_params(1),
        name="kv_proj",
    )(mem, w)


def _xattn_kernel(x_ref, kv_ref, wq_ref, wo_ref, g_ref, b_ref, out_ref, *, alpha):
    x = x_ref[...]
    d = x.shape[1]
    hd = d // XATTN_HEADS
    q = jnp.dot(x.astype(BF16), wq_ref[...].astype(BF16), preferred_element_type=F32).astype(BF16)
    heads = []
    for h in range(XATTN_HEADS):
        k = kv_ref[:, h * hd:(h + 1) * hd]
        v = kv_ref[:, d + h * hd:d + (h + 1) * hd]
        s = _mm_nt(q[:, h * hd:(h + 1) * hd], k) * (hd ** -0.5)
        e = jnp.exp(s - jnp.max(s, axis=-1, keepdims=True))
        p = e / jnp.sum(e, axis=-1, keepdims=True)
        heads.append(jnp.dot(p.astype(BF16), v, preferred_element_type=F32).astype(BF16))
    o = jnp.concatenate(heads, axis=1)
    att = jnp.dot(o, wo_ref[...].astype(BF16), preferred_element_type=F32)
    out_ref[...] = _layer_norm(alpha * x + att, g_ref[...], b_ref[...])


def _xattn(x, kv, wq, wo, g, b, *, alpha, batch, tl=1024):
    t, d = x.shape
    nl = t // batch // tl
    n_mem = kv.shape[0] // batch
    row = lambda bi, l: (bi * nl + l, 0)
    return pl.pallas_call(
        functools.partial(_xattn_kernel, alpha=alpha),
        grid=(batch, nl),
        in_specs=[pl.BlockSpec((tl, d), row),
                  pl.BlockSpec((n_mem, kv.shape[1]), lambda bi, l: (bi, 0)),
                  _const_spec(wq.shape), _const_spec(wo.shape),
                  _const_spec(g.shape), _const_spec(b.shape)],
        out_specs=pl.BlockSpec((tl, d), row),
        out_shape=jax.ShapeDtypeStruct((t, d), F32),
        compiler_params=_params(2),
        name="xattn",
    )(x, kv, wq, wo, g, b)


def _row(v):
    return v.reshape(1, -1).astype(F32)


def _pad_to(v, shape):
    return jnp.pad(v, tuple((0, n - s) for s, n in zip(v.shape, shape)))


def _interleave_halves(w):
    k, n = w.shape
    return w.reshape(k, 2, n // 2 // LANES, LANES).transpose(0, 2, 1, 3).reshape(k, n)


def _layer(x, mem, p, *, batch, alpha):
    d = x.shape[1]
    kw = GDN_HEADS * GDN_DK
    vw = GDN_HEADS * GDN_DV
    w_in = p["w_in"]
    o0 = 2 * kw + vw
    o1 = o0 + vw
    o2 = o1 + 2 * GDN_HEADS
    o3 = o2 + 2 * d
    w_ab = w_in[:, o1:o2]
    wabt = _pad_to(w_ab[:, :GDN_HEADS].T, (BF16_ROWS, d)).astype(BF16)
    alog = _row(p["gdn_a_log"])
    dtb = _row(p["gdn_dt_bias"])
    w_all = jnp.concatenate([w_in[:, :o1], _interleave_halves(w_in[:, o2:o3]), w_in[:, o3:],
                             _pad_to(w_ab, (d, LANES))], axis=1).astype(BF16)
    assert o0 % vw == 0 and o1 % (2 * d) == 0 and (o1 + 4 * d) % LANES == 0
    wqkv = _Cols(w_all, o0, 0)
    wz = _Cols(w_all, vw, o0 // vw)
    wglu = _Cols(w_all, 2 * d, o1 // (2 * d))
    wgate = _Cols(w_all, 2 * d, o1 // (2 * d) + 1)
    wab = _Cols(w_all, LANES, (o1 + 4 * d) // LANES)

    x, q, k, kb, qg, kd, rhs, gc, gl, gcr = _ffn_prep(
        x, p["ffn1_wg"].astype(BF16), p["ffn1_wu"].astype(BF16), p["ffn1_wd"].astype(BF16),
        _row(p["ln1_g"]), _row(p["ln1_b"]), wqkv, wab, wabt,
        p["gdn_conv_qkv"].astype(F32), _pad_to(alog, (1, LANES)), _pad_to(dtb, (1, LANES)),
        _pad_to(alog.reshape(-1, 1), (BF16_ROWS, 1)), _pad_to(dtb.reshape(-1, 1), (BF16_ROWS, 1)),
        alpha=alpha, batch=batch)
    o, c, sg = _mixer(
        x, q, k, kb, qg, kd, rhs, gc, gl, gcr, _row(p["gdn_norm_g"]), wz, wglu, wgate,
        p["conv_dw_w"].astype(F32), _row(p["conv_dw_b"]), _row(p["conv_ln_g"]), _row(p["conv_ln_b"]),
        batch=batch)
    x = _merge(x, o, c, sg, p["w_gdn_out"], p["w_conv_out"], _row(p["b_conv_out"]), p["w_mix_out"],
               _row(p["ln2_g"]), _row(p["ln2_b"]), alpha=alpha)

    kv = _kv_proj(mem, p["w_xkv"])
    x = _xattn(x, kv, p["w_xq"], p["w_xo"], _row(p["ln3_g"]), _row(p["ln3_b"]), alpha=alpha, batch=batch)

    return _ffn_ln(x, p["ffn2_wg"], p["ffn2_wu"], p["ffn2_wd"],
                   _row(p["ln4_g"]), _row(p["ln4_b"]), alpha=alpha)


def kernel(x, mem, ffn1_wg, ffn1_wu, ffn1_wd, ln1_g, ln1_b, w_in, gdn_conv_qkv, gdn_a_log, gdn_dt_bias, gdn_norm_g, w_gdn_out, conv_dw_w, conv_dw_b, conv_ln_g, conv_ln_b, w_conv_out, b_conv_out, w_mix_out, ln2_g, ln2_b, w_xq, w_xkv, w_xo, ln3_g, ln3_b, ffn2_wg, ffn2_wu, ffn2_wd, ln4_g, ln4_b):
    weights = dict(
        ffn1_wg=ffn1_wg, ffn1_wu=ffn1_wu, ffn1_wd=ffn1_wd, ln1_g=ln1_g, ln1_b=ln1_b, w_in=w_in,
        gdn_conv_qkv=gdn_conv_qkv, gdn_a_log=gdn_a_log, gdn_dt_bias=gdn_dt_bias,
        gdn_norm_g=gdn_norm_g, w_gdn_out=w_gdn_out, conv_dw_w=conv_dw_w, conv_dw_b=conv_dw_b,
        conv_ln_g=conv_ln_g, conv_ln_b=conv_ln_b, w_conv_out=w_conv_out, b_conv_out=b_conv_out,
        w_mix_out=w_mix_out, ln2_g=ln2_g, ln2_b=ln2_b, w_xq=w_xq, w_xkv=w_xkv, w_xo=w_xo,
        ln3_g=ln3_g, ln3_b=ln3_b, ffn2_wg=ffn2_wg, ffn2_wu=ffn2_wu, ffn2_wd=ffn2_wd,
        ln4_g=ln4_g, ln4_b=ln4_b)
    batch, seq, d = x.shape
    depth = ffn1_wg.shape[0]
    alpha = (2 * depth) ** 0.25
    h = x.reshape(batch * seq, d)
    m = mem.reshape(-1, d)
    for i in range(depth):
        h = _layer(h, m, {k: v[i] for k, v in weights.items()}, batch=batch, alpha=alpha)
    return h.reshape(batch, seq, d)
```

```python
import functools

import jax
import jax.numpy as jnp
from jax import lax
from jax.experimental import pallas as pl
from jax.experimental.pallas import tpu as pltpu

F32 = jnp.float32
BF16 = jnp.bfloat16

LN_EPS = 1e-5
RMS_EPS = 1e-6
GDN_HEADS = 8
GDN_DK = 128
GDN_DV = 128
SHORT_CONV = 4
CHUNK = 64
CONV_WIDTH = 31
XATTN_HEADS = 4
N_MEM = 256

LANES = 128
SUBLANES = 8
BF16_ROWS = 16
VMEM_LIMIT = 56 * 1024 * 1024
CONV_HALO = 5 * SUBLANES


def _params(n_axes):
    return pltpu.CompilerParams(dimension_semantics=("arbitrary",) * n_axes,
                                vmem_limit_bytes=VMEM_LIMIT)


def _const_spec(shape):
    zeros = (0,) * len(shape)
    return pl.BlockSpec(shape, lambda *_: zeros, pipeline_mode=pl.Buffered(1))


def _mm_nt(a, b):
    return lax.dot_general(a.astype(BF16), b.astype(BF16), (((1,), (1,)), ((), ())),
                           preferred_element_type=F32)


def _split3(v):
    h1 = v.astype(BF16)
    r1 = v - h1.astype(F32)
    h2 = r1.astype(BF16)
    h3 = (r1 - h2.astype(F32)).astype(BF16)
    return h1, h2, h3


def _layer_norm(v, g, b):
    mu = jnp.mean(v, axis=-1, keepdims=True)
    d = v - mu
    var = jnp.mean(d * d, axis=-1, keepdims=True)
    return d * lax.rsqrt(var + LN_EPS) * g + b


def _silu(v):
    return v * jax.nn.sigmoid(v)


def _softplus(v):
    return jnp.maximum(v, 0.0) + jnp.log1p(jnp.exp(-jnp.abs(v)))


def _ffn_residual(x, wg_ref, wu_ref, wd_ref, *, alpha, fc):
    xb = x.astype(BF16)
    acc = jnp.zeros(x.shape, F32)
    for j in range(wg_ref.shape[1] // fc):
        sl = slice(j * fc, (j + 1) * fc)
        gate = jnp.dot(xb, wg_ref[:, sl].astype(BF16), preferred_element_type=F32)
        up = jnp.dot(xb, wu_ref[:, sl].astype(BF16), preferred_element_type=F32)
        h = (_silu(gate) * up).astype(BF16)
        acc = acc + jnp.dot(h, wd_ref[sl, :].astype(BF16), preferred_element_type=F32)
    return alpha * x + 0.5 * acc


def _ffn_ln_kernel(x_ref, wg_ref, wu_ref, wd_ref, g_ref, b_ref, o_ref, *, alpha, fc):
    y = _ffn_residual(x_ref[...], wg_ref, wu_ref, wd_ref, alpha=alpha, fc=fc)
    o_ref[...] = _layer_norm(y, g_ref[...], b_ref[...])


def _ffn_ln(x, wg, wu, wd, g, b, *, alpha, tm=512, fc=256):
    t, d = x.shape
    consts = (wg, wu, wd, g, b)
    return pl.pallas_call(
        functools.partial(_ffn_ln_kernel, alpha=alpha, fc=fc),
        grid=(t // tm,),
        in_specs=[pl.BlockSpec((tm, d), lambda i: (i, 0))] + [_const_spec(w.shape) for w in consts],
        out_specs=pl.BlockSpec((tm, d), lambda i: (i, 0)),
        out_shape=jax.ShapeDtypeStruct((t, d), F32),
        compiler_params=_params(1),
        name="ffn_ln",
    )(x, *consts)


def _ffn_prep_kernel(x_ref, wg_ref, wu_ref, wd_ref, lg_ref, lb_ref, wqkv_ref, wab_ref, wabt_ref, cw_ref,
                     alog_r_ref, dtb_r_ref, alog_c_ref, dtb_c_ref,
                     x1_ref, q_ref, k_ref, kb_ref, qg_ref, kd_ref, rhs_ref, gc_ref, gl_ref, gcr_ref,
                     ext_ref, *, alpha, fc, tl):
    halo = SUBLANES
    kw = GDN_HEADS * GDN_DK
    nc = tl // CHUNK

    @pl.when(pl.program_id(1) == 0)
    def _():
        ext_ref[0:halo, :] = jnp.zeros((halo, ext_ref.shape[1]), F32)

    y = _ffn_residual(x_ref[...], wg_ref, wu_ref, wd_ref, alpha=alpha, fc=fc)
    x1 = _layer_norm(y, lg_ref[...], lb_ref[...])
    x1_ref[...] = x1
    xb = x1.astype(BF16)
    ext_ref[halo:halo + tl, :] = jnp.dot(xb, wqkv_ref[...], preferred_element_type=F32)
    ab = jnp.dot(xb, wab_ref[...], preferred_element_type=F32)
    arow = lax.dot_general(wabt_ref[...], xb, (((1,), (1,)), ((), ())), preferred_element_type=F32)

    row = lax.broadcasted_iota(jnp.int32, (tl, tl), 0)
    col = lax.broadcasted_iota(jnp.int32, (tl, tl), 1)
    same = (row // CHUNK) == (col // CHUNK)
    tri_b = jnp.where(same & (row >= col), 1.0, 0.0).astype(BF16)
    utri_b = jnp.where(same & (row <= col), 1.0, 0.0).astype(BF16)
    g_col = -jnp.exp(alog_r_ref[...]) * _softplus(ab + dtb_r_ref[...])
    gc_col = sum(jnp.dot(tri_b, part, preferred_element_type=F32) for part in _split3(g_col))
    gl_col = jnp.concatenate(
        [jnp.broadcast_to(gc_col[(c + 1) * CHUNK - 1:(c + 1) * CHUNK, :], (CHUNK, LANES))
         for c in range(nc)], axis=0)
    g_row = -jnp.exp(alog_c_ref[...]) * _softplus(arow + dtb_c_ref[...])
    gcr_ref[...] = sum(jnp.dot(part, utri_b, preferred_element_type=F32) for part in _split3(g_row))
    gc_ref[...] = gc_col
    gl_ref[...] = gl_col
    beta_all = jax.nn.sigmoid(ab)
    e_all = jnp.exp(gc_col)
    d_all = jnp.exp(gl_col - gc_col)

    def conv_silu(c0):
        acc = None
        for j in reversed(range(SHORT_CONV)):
            r0 = halo - (SHORT_CONV - 1) + j
            term = cw_ref[j:j + 1, c0:c0 + LANES] * ext_ref[r0:r0 + tl, c0:c0 + LANES]
            acc = term if acc is None else acc + term
        return _silu(acc)

    for h in range(GDN_HEADS):
        hs = slice(h * GDN_DK, (h + 1) * GDN_DK)
        q = conv_silu(h * GDN_DK)
        k = conv_silu(kw + h * GDN_DK)
        v = conv_silu(2 * kw + h * GDN_DV)
        q = q * (lax.rsqrt(jnp.sum(q * q, axis=-1, keepdims=True) + RMS_EPS) * (GDN_DK ** -0.5))
        k = k * lax.rsqrt(jnp.sum(k * k, axis=-1, keepdims=True) + RMS_EPS)
        beta = beta_all[:, GDN_HEADS + h:GDN_HEADS + h + 1]
        e_g = e_all[:, h:h + 1]
        kb = k * beta
        q_ref[:, hs] = q.astype(BF16)
        k_ref[:, hs] = k.astype(BF16)
        kb_ref[:, hs] = kb.astype(BF16)
        qg_ref[:, hs] = (q * e_g).astype(BF16)
        kd_ref[:, hs] = (k * d_all[:, h:h + 1]).astype(BF16)
        rhs_ref[:, 2 * h * GDN_DV:(2 * h + 1) * GDN_DV] = (v * beta).astype(BF16)
        rhs_ref[:, (2 * h + 1) * GDN_DV:(2 * h + 2) * GDN_DV] = (kb * e_g).astype(BF16)

    ext_ref[0:halo, :] = ext_ref[tl:tl + halo, :]


def _ffn_prep(x, wg, wu, wd, lg, lb, wqkv, wab, wabt, cw, alog_r, dtb_r, alog_c, dtb_c,
              *, alpha, batch, tl=256, fc=2816):
    t, d = x.shape
    nl = t // batch // tl
    kw = GDN_HEADS * GDN_DK
    vw = GDN_HEADS * GDN_DV
    nr = wabt.shape[0]
    row = lambda b, l: (b * nl + l, 0)
    consts = (wg, wu, wd, lg, lb, wqkv, wab, wabt, cw, alog_r, dtb_r, alog_c, dtb_c)
    wide = lambda n, dt: (pl.BlockSpec((tl, n), row), jax.ShapeDtypeStruct((t, n), dt))
    outs = [wide(d, F32)] + [wide(kw, BF16)] * 5 + [wide(2 * vw, BF16), wide(LANES, F32), wide(LANES, F32),
                                                    (pl.BlockSpec((nr, tl), lambda b, l: (0, b * nl + l)),
                                                     jax.ShapeDtypeStruct((nr, t), F32))]
    return pl.pallas_call(
        functools.partial(_ffn_prep_kernel, alpha=alpha, fc=fc, tl=tl),
        grid=(batch, nl),
        in_specs=[pl.BlockSpec((tl, d), row)] + [_const_spec(w.shape) for w in consts],
        out_specs=[o[0] for o in outs],
        out_shape=[o[1] for o in outs],
        scratch_shapes=[pltpu.VMEM((SUBLANES + tl, wqkv.shape[1]), F32)],
        compiler_params=_params(2),
        name="ffn_prep",
    )(x, *consts)


def _conv31_block(w_ref, ext_ref, y_ref, c0, t0, *, rb, anchor=None):
    cols = slice(c0, c0 + LANES)
    acc = None
    for r in range(SUBLANES):
        z = None
        for q in range((CONV_WIDTH - 1 - r) // SUBLANES + 1):
            j = CONV_WIDTH - 1 - (SUBLANES * q + r)
            lo = CONV_HALO + t0 - SUBLANES * (q + 1)
            w = w_ref[j:j + 1, cols]
            if anchor is not None:
                w = jnp.where(anchor[0], w, anchor[1])
            term = w * ext_ref[lo:lo + rb + SUBLANES, cols]
            z = term if z is None else z + term
        sh = z[SUBLANES - r:SUBLANES - r + rb]
        acc = sh if acc is None else acc + sh
    y_ref[t0:t0 + rb, cols] = acc


def _mixer_kernel(x_ref, q_ref, k_ref, kb_ref, qg_ref, kd_ref, rhs_ref, gc_ref, gl_ref, gcr_ref,
                  ng_ref, wz_ref, wglu_ref, wgate_ref, cw_ref, cb_ref, lg_ref, lb_ref, ones_ref,
                  o_ref, cact_ref, sg_ref, s_ref, ext_ref, y_ref, *, tl, rb, plain_blocks, anchor_every):
    nc = tl // CHUNK
    heads = range(GDN_HEADS)
    ch = ext_ref.shape[1]

    @pl.when(pl.program_id(1) == 0)
    def _():
        s_ref[...] = jnp.zeros(s_ref.shape, F32)
        ext_ref[0:CONV_HALO, :] = jnp.zeros((CONV_HALO, ch), F32)

    xb = x_ref[...].astype(BF16)
    for c0 in range(0, ch, LANES):
        glu = jnp.dot(xb, wglu_ref[:, 2 * c0:2 * c0 + 2 * LANES], preferred_element_type=F32)
        ext_ref[CONV_HALO:CONV_HALO + tl, c0:c0 + LANES] = glu[:, :LANES] * jax.nn.sigmoid(glu[:, LANES:])
    blocks = [(c0, t0) for c0 in range(0, ch, LANES) for t0 in range(0, tl, rb)]
    late = blocks[plain_blocks:]
    for c0, t0 in blocks[:plain_blocks]:
        _conv31_block(cw_ref, ext_ref, y_ref, c0, t0, rb=rb)
    always = ones_ref[...] > 0.5
    ticks = [0]

    def tick(val):
        ticks[0] += 1
        if late and ticks[0] % anchor_every == 0:
            c0, t0 = late.pop(0)
            _conv31_block(cw_ref, ext_ref, y_ref, c0, t0, rb=rb, anchor=(always, val[0:1, 0:LANES]))
        return val

    zs = _silu(jnp.dot(xb, wz_ref[...], preferred_element_type=F32))
    sg_ref[...] = jax.nn.sigmoid(
        jnp.dot(xb, wgate_ref[...], preferred_element_type=F32)).astype(sg_ref.dtype)

    row = lax.broadcasted_iota(jnp.int32, (tl, tl), 0)
    col = lax.broadcasted_iota(jnp.int32, (tl, tl), 1)
    same_b = jnp.where((row // CHUNK) == (col // CHUNK), 1.0, 0.0).astype(BF16)
    pi = lax.broadcasted_iota(jnp.int32, (CHUNK, tl), 0)
    lane = lax.broadcasted_iota(jnp.int32, (CHUNK, tl), 1)
    lc = lane // CHUNK
    lj = lane % CHUNK
    tri_p = pi >= lj
    strict_p = pi > lj
    eye_p = jnp.where(pi == lj, 1.0, 0.0)
    base = 16
    base_p = (pi // base) == (lj // base)

    def pack(full):
        p = full[0:CHUNK]
        for c in range(1, nc):
            p = jnp.where(lc == c, full[c * CHUNK:(c + 1) * CHUNK], p)
        return p

    def blockdiag(p):
        return jnp.concatenate([p.astype(BF16)] * nc, axis=0) * same_b

    def pmm(a, b):
        return jnp.dot(a.astype(BF16), blockdiag(b), preferred_element_type=F32)

    m, a = [], []
    for h in heads:
        hs = slice(h * GDN_DK, (h + 1) * GDN_DK)
        k = k_ref[:, hs]
        g_pack = pack(jnp.broadcast_to(gc_ref[:, h:h + 1], (tl, tl)))
        diff = g_pack - gcr_ref[h:h + 1, :]
        decay = jnp.where(tri_p, jnp.exp(jnp.where(tri_p, diff, 0.0)), 0.0)
        m.append(jnp.where(strict_p, pack(_mm_nt(kb_ref[:, hs], k)) * decay, 0.0))
        a.append(tick(pack(_mm_nt(q_ref[:, hs], k)) * decay))

    n = [jnp.where(base_p, -m[h], 0.0) for h in heads]
    inv = [eye_p + n[h] for h in heads]
    for _ in range(3):
        n = [tick(pmm(n[h], n[h])) for h in heads]
        inv = [tick(inv[h] + pmm(n[h], inv[h])) for h in heads]
    s = base
    while s < CHUNK:
        lower_left = ((pi // (2 * s)) == (lj // (2 * s))) & ((pi // s) == (lj // s) + 1)
        e = [tick(pmm(jnp.where(lower_left, m[h], 0.0), inv[h])) for h in heads]
        inv = [tick(inv[h] - pmm(inv[h], e[h])) for h in heads]
        s *= 2
    xb16 = [jnp.dot(blockdiag(inv[h]), rhs_ref[:, 2 * h * GDN_DV:(2 * h + 2) * GDN_DV],
                    preferred_element_type=F32).astype(BF16) for h in heads]

    ax = [tick(jnp.dot(blockdiag(a[h]), xb16[h], preferred_element_type=F32)) for h in heads]
    kx = [[lax.dot_general(kd_ref[c * CHUNK:(c + 1) * CHUNK, h * GDN_DK:(h + 1) * GDN_DK],
                           xb16[h][c * CHUNK:(c + 1) * CHUNK], (((0,), (0,)), ((), ())),
                           preferred_element_type=F32) for c in range(nc)] for h in heads]
    state = [s_ref[h] for h in heads]
    outs = [[] for _ in heads]
    for c in range(nc):
        r = slice(c * CHUNK, (c + 1) * CHUNK)
        for h in heads:
            hs = slice(h * GDN_DK, (h + 1) * GDN_DK)
            sb = state[h].astype(BF16)
            r_c = (qg_ref[r, hs].astype(F32) - ax[h][r, GDN_DV:]).astype(BF16)
            outs[h].append(jnp.dot(r_c, sb, preferred_element_type=F32) + ax[h][r, :GDN_DV])
            state[h] = (state[h] * jnp.exp(gl_ref[c * CHUNK:c * CHUNK + 1, h:h + 1])
                        - jnp.dot(kx[h][c][:, GDN_DV:].astype(BF16), sb, preferred_element_type=F32)
                        + kx[h][c][:, :GDN_DV])
            tick(state[h])

    for c0, t0 in late:
        _conv31_block(cw_ref, ext_ref, y_ref, c0, t0, rb=rb)
    cact_ref[...] = _silu(_layer_norm(y_ref[...] + cb_ref[...], lg_ref[...], lb_ref[...])
                          ).astype(cact_ref.dtype)
    ext_ref[0:CONV_HALO, :] = ext_ref[tl:tl + CONV_HALO, :]
    for h in heads:
        hs = slice(h * GDN_DV, (h + 1) * GDN_DV)
        s_ref[h] = state[h]
        o = jnp.concatenate(outs[h], axis=0)
        o = o * lax.rsqrt(jnp.mean(o * o, axis=-1, keepdims=True) + RMS_EPS) * ng_ref[...]
        o_ref[:, hs] = (o * zs[:, hs]).astype(o_ref.dtype)


def _mixer(x, q, k, kb, qg, kd, rhs, gc, gl, gcr, ng, wz, wglu, wgate, cw, cb, lg, lb,
           *, batch, tl=256, rb=64, plain_blocks=20, anchor_every=8):
    t = x.shape[0]
    nl = t // batch // tl
    vw, ch, ns = wz.shape[1], wglu.shape[1] // 2, wgate.shape[1]
    row = lambda b, l: (b * nl + l, 0)
    tiles = (x, q, k, kb, qg, kd, rhs, gc, gl)
    consts = (ng, wz, wglu, wgate, cw, cb, lg, lb, jnp.ones((1, LANES), F32))
    return pl.pallas_call(
        functools.partial(_mixer_kernel, tl=tl, rb=rb, plain_blocks=plain_blocks,
                          anchor_every=anchor_every),
        grid=(batch, nl),
        in_specs=([pl.BlockSpec((tl, v.shape[1]), row) for v in tiles]
                  + [pl.BlockSpec((gcr.shape[0], tl), lambda b, l: (0, b * nl + l))]
                  + [_const_spec(w.shape) for w in consts]),
        out_specs=[pl.BlockSpec((tl, vw), row), pl.BlockSpec((tl, ch), row), pl.BlockSpec((tl, ns), row)],
        out_shape=[jax.ShapeDtypeStruct((t, vw), BF16), jax.ShapeDtypeStruct((t, ch), BF16),
                   jax.ShapeDtypeStruct((t, ns), BF16)],
        scratch_shapes=[pltpu.VMEM((GDN_HEADS, GDN_DK, GDN_DV), F32),
                        pltpu.VMEM((CONV_HALO + tl, ch), F32), pltpu.VMEM((tl, ch), F32)],
        compiler_params=_params(2),
        name="mixer",
    )(*tiles, gcr, *consts)


def _merge_kernel(x_ref, o_ref, c_ref, sg_ref, wa_ref, wc_ref, bc_ref, wm_ref, g_ref, b_ref,
                  out_ref, *, alpha, groups):
    d = x_ref.shape[1]
    wa, wc, wm = (w[...].astype(BF16) for w in (wa_ref, wc_ref, wm_ref))
    rows = x_ref.shape[0] // groups
    for i in range(groups):
        r = slice(i * rows, (i + 1) * rows)
        y_a = jnp.dot(o_ref[r, :], wa, preferred_element_type=F32)
        y_c = jnp.dot(c_ref[r, :], wc, preferred_element_type=F32) + bc_ref[...]
        y = sg_ref[r, :d] * y_a + sg_ref[r, d:] * y_c
        mixed = jnp.dot(y.astype(BF16), wm, preferred_element_type=F32)
        out_ref[r, :] = _layer_norm(alpha * x_ref[r, :] + mixed, g_ref[...], b_ref[...])


def _merge(x, o, c, sg, wa, wc, bc, wm, g, b, *, alpha, tm=1024, groups=4):
    t, d = x.shape
    row = lambda i: (i, 0)
    return pl.pallas_call(
        functools.partial(_merge_kernel, alpha=alpha, groups=groups),
        grid=(t // tm,),
        in_specs=[pl.BlockSpec((tm, d), row), pl.BlockSpec((tm, o.shape[1]), row),
                  pl.BlockSpec((tm, c.shape[1]), row), pl.BlockSpec((tm, sg.shape[1]), row),
                  _const_spec(wa.shape), _const_spec(wc.shape), _const_spec(bc.shape),
                  _const_spec(wm.shape), _const_spec(g.shape), _const_spec(b.shape)],
        out_specs=pl.BlockSpec((tm, d), row),
        out_shape=jax.ShapeDtypeStruct((t, d), F32),
        compiler_params=_params(1),
        name="merge",
    )(x, o, c, sg, wa, wc, bc, wm, g, b)


def _kv_proj_kernel(m_ref, w_ref, o_ref):
    o_ref[...] = jnp.dot(m_ref[...].astype(BF16), w_ref[...].astype(BF16),
                         preferred_element_type=F32).astype(o_ref.dtype)


def _kv_proj(mem, w, *, tm=256):
    t, d = mem.shape
    n = w.shape[1]
    return pl.pallas_call(
        _kv_proj_kernel,
        grid=(t // tm,),
        in_specs=[pl.BlockSpec((tm, d), lambda i: (i, 0)), _const_spec(w.shape)],
        out_specs=pl.BlockSpec((tm, n), lambda i: (i, 0)),
        out_shape=jax.ShapeDtypeStruct((t, n), BF16),
        compiler_params=_params(1),
        name="kv_proj",
    )(mem, w)


def _xattn_kernel(x_ref, kv_ref, wq_ref, wo_ref, g_ref, b_ref, out_ref, *, alpha, groups):
    d = x_ref.shape[1]
    hd = d // XATTN_HEADS
    wq, wo = wq_ref[...].astype(BF16), wo_ref[...].astype(BF16)
    rows = x_ref.shape[0] // groups
    for i in range(groups):
        r = slice(i * rows, (i + 1) * rows)
        x = x_ref[r, :]
        q = jnp.dot(x.astype(BF16), wq, preferred_element_type=F32).astype(BF16)
        heads = []
        for h in range(XATTN_HEADS):
            k = kv_ref[:, h * hd:(h + 1) * hd]
            v = kv_ref[:, d + h * hd:d + (h + 1) * hd]
            s = _mm_nt(q[:, h * hd:(h + 1) * hd], k) * (hd ** -0.5)
            e = jnp.exp(s - jnp.max(s, axis=-1, keepdims=True))
            p = e / jnp.sum(e, axis=-1, keepdims=True)
            heads.append(jnp.dot(p.astype(BF16), v, preferred_element_type=F32).astype(BF16))
        o = jnp.concatenate(heads, axis=1)
        att = jnp.dot(o, wo, preferred_element_type=F32)
        out_ref[r, :] = _layer_norm(alpha * x + att, g_ref[...], b_ref[...])


def _xattn(x, kv, wq, wo, g, b, *, alpha, batch, tl=1024, groups=2):
    t, d = x.shape
    nl = t // batch // tl
    n_mem = kv.shape[0] // batch
    row = lambda bi, l: (bi * nl + l, 0)
    return pl.pallas_call(
        functools.partial(_xattn_kernel, alpha=alpha, groups=groups),
        grid=(batch, nl),
        in_specs=[pl.BlockSpec((tl, d), row),
                  pl.BlockSpec((n_mem, kv.shape[1]), lambda bi, l: (bi, 0)),
                  _const_spec(wq.shape), _const_spec(wo.shape),
                  _const_spec(g.shape), _const_spec(b.shape)],
        out_specs=pl.BlockSpec((tl, d), row),
        out_shape=jax.ShapeDtypeStruct((t, d), F32),
        compiler_params=_params(2),
        name="xattn",
    )(x, kv, wq, wo, g, b)


def _row(v):
    return v.reshape(1, -1).astype(F32)


def _pad_to(v, shape):
    return jnp.pad(v, tuple((0, n - s) for s, n in zip(v.shape, shape)))


def _interleave_halves(w):
    k, n = w.shape
    return w.reshape(k, 2, n // 2 // LANES, LANES).transpose(0, 2, 1, 3).reshape(k, n)


def _layer(x, mem, p, *, batch, alpha):
    d = x.shape[1]
    kw = GDN_HEADS * GDN_DK
    vw = GDN_HEADS * GDN_DV
    w_in = p["w_in"]
    o0 = 2 * kw + vw
    o1 = o0 + vw
    o2 = o1 + 2 * GDN_HEADS
    o3 = o2 + 2 * d
    w_ab = w_in[:, o1:o2]
    wab = _pad_to(w_ab, (d, LANES)).astype(BF16)
    wabt = _pad_to(w_ab[:, :GDN_HEADS].T, (BF16_ROWS, d)).astype(BF16)
    alog = _row(p["gdn_a_log"])
    dtb = _row(p["gdn_dt_bias"])

    x, q, k, kb, qg, kd, rhs, gc, gl, gcr = _ffn_prep(
        x, p["ffn1_wg"].astype(BF16), p["ffn1_wu"].astype(BF16), p["ffn1_wd"].astype(BF16),
        _row(p["ln1_g"]), _row(p["ln1_b"]), w_in[:, :o0].astype(BF16), wab, wabt,
        p["gdn_conv_qkv"].astype(F32), _pad_to(alog, (1, LANES)), _pad_to(dtb, (1, LANES)),
        _pad_to(alog.reshape(-1, 1), (BF16_ROWS, 1)), _pad_to(dtb.reshape(-1, 1), (BF16_ROWS, 1)),
        alpha=alpha, batch=batch)
    o, c, sg = _mixer(
        x, q, k, kb, qg, kd, rhs, gc, gl, gcr, _row(p["gdn_norm_g"]),
        w_in[:, o0:o1].astype(BF16), _interleave_halves(w_in[:, o2:o3]).astype(BF16),
        w_in[:, o3:].astype(BF16),
        p["conv_dw_w"].astype(F32), _row(p["conv_dw_b"]), _row(p["conv_ln_g"]), _row(p["conv_ln_b"]),
        batch=batch)
    x = _merge(x, o, c, sg, p["w_gdn_out"], p["w_conv_out"], _row(p["b_conv_out"]), p["w_mix_out"],
               _row(p["ln2_g"]), _row(p["ln2_b"]), alpha=alpha)

    kv = _kv_proj(mem, p["w_xkv"])
    x = _xattn(x, kv, p["w_xq"], p["w_xo"], _row(p["ln3_g"]), _row(p["ln3_b"]), alpha=alpha, batch=batch)

    return _ffn_ln(x, p["ffn2_wg"], p["ffn2_wu"], p["ffn2_wd"],
                   _row(p["ln4_g"]), _row(p["ln4_b"]), alpha=alpha)


def kernel(x, mem, ffn1_wg, ffn1_wu, ffn1_wd, ln1_g, ln1_b, w_in, gdn_conv_qkv, gdn_a_log, gdn_dt_bias, gdn_norm_g, w_gdn_out, conv_dw_w, conv_dw_b, conv_ln_g, conv_ln_b, w_conv_out, b_conv_out, w_mix_out, ln2_g, ln2_b, w_xq, w_xkv, w_xo, ln3_g, ln3_b, ffn2_wg, ffn2_wu, ffn2_wd, ln4_g, ln4_b):
    weights = dict(
        ffn1_wg=ffn1_wg, ffn1_wu=ffn1_wu, ffn1_wd=ffn1_wd, ln1_g=ln1_g, ln1_b=ln1_b, w_in=w_in,
        gdn_conv_qkv=gdn_conv_qkv, gdn_a_log=gdn_a_log, gdn_dt_bias=gdn_dt_bias,
        gdn_norm_g=gdn_norm_g, w_gdn_out=w_gdn_out, conv_dw_w=conv_dw_w, conv_dw_b=conv_dw_b,
        conv_ln_g=conv_ln_g, conv_ln_b=conv_ln_b, w_conv_out=w_conv_out, b_conv_out=b_conv_out,
        w_mix_out=w_mix_out, ln2_g=ln2_g, ln2_b=ln2_b, w_xq=w_xq, w_xkv=w_xkv, w_xo=w_xo,
        ln3_g=ln3_g, ln3_b=ln3_b, ffn2_wg=ffn2_wg, ffn2_wu=ffn2_wu, ffn2_wd=ffn2_wd,
        ln4_g=ln4_g, ln4_b=ln4_b)
    batch, seq, d = x.shape
    depth = ffn1_wg.shape[0]
    alpha = (2 * depth) ** 0.25
    h = x.reshape(batch * seq, d)
    m = mem.reshape(-1, d)
    for i in range(depth):
        h = _layer(h, m, {k: v[i] for k, v in weights.items()}, batch=batch, alpha=alpha)
    return h.reshape(batch, seq, d)
```

```python
import functools

import jax
import jax.numpy as jnp
from jax import lax
from jax.experimental import pallas as pl
from jax.experimental.pallas import tpu as pltpu

F32 = jnp.float32
BF16 = jnp.bfloat16

LN_EPS = 1e-5
RMS_EPS = 1e-6
GDN_HEADS = 8
GDN_DK = 128
GDN_DV = 128
SHORT_CONV = 4
CHUNK = 64
CONV_WIDTH = 31
XATTN_HEADS = 4
N_MEM = 256

LANES = 128
SUBLANES = 8
BF16_ROWS = 16
VMEM_LIMIT = 56 * 1024 * 1024
CONV_HALO = 5 * SUBLANES


def _params(n_axes):
    return pltpu.CompilerParams(dimension_semantics=("arbitrary",) * n_axes,
                                vmem_limit_bytes=VMEM_LIMIT)


def _const_spec(shape):
    zeros = (0,) * len(shape)
    return pl.BlockSpec(shape, lambda *_: zeros, pipeline_mode=pl.Buffered(1))


def _mm_nt(a, b):
    return lax.dot_general(a.astype(BF16), b.astype(BF16), (((1,), (1,)), ((), ())),
                           preferred_element_type=F32)


def _split3(v):
    h1 = v.astype(BF16)
    r1 = v - h1.astype(F32)
    h2 = r1.astype(BF16)
    h3 = (r1 - h2.astype(F32)).astype(BF16)
    return h1, h2, h3


def _layer_norm(v, g, b):
    mu = jnp.mean(v, axis=-1, keepdims=True)
    d = v - mu
    var = jnp.mean(d * d, axis=-1, keepdims=True)
    return d * lax.rsqrt(var + LN_EPS) * g + b


def _silu(v):
    return v * jax.nn.sigmoid(v)


def _softplus(v):
    return jnp.maximum(v, 0.0) + jnp.log1p(jnp.exp(-jnp.abs(v)))


def _ffn_residual(x, wg_ref, wu_ref, wd_ref, *, alpha, fc):
    xb = x.astype(BF16)
    acc = jnp.zeros(x.shape, F32)
    for j in range(wg_ref.shape[1] // fc):
        sl = slice(j * fc, (j + 1) * fc)
        gate = jnp.dot(xb, wg_ref[:, sl].astype(BF16), preferred_element_type=F32)
        up = jnp.dot(xb, wu_ref[:, sl].astype(BF16), preferred_element_type=F32)
        h = (_silu(gate) * up).astype(BF16)
        acc = acc + jnp.dot(h, wd_ref[sl, :].astype(BF16), preferred_element_type=F32)
    return alpha * x + 0.5 * acc


def _ffn_ln_kernel(x_ref, wg_ref, wu_ref, wd_ref, g_ref, b_ref, o_ref, *, alpha, fc):
    y = _ffn_residual(x_ref[...], wg_ref, wu_ref, wd_ref, alpha=alpha, fc=fc)
    o_ref[...] = _layer_norm(y, g_ref[...], b_ref[...])


def _ffn_ln(x, wg, wu, wd, g, b, *, alpha, tm=512, fc=256):
    t, d = x.shape
    consts = (wg, wu, wd, g, b)
    return pl.pallas_call(
        functools.partial(_ffn_ln_kernel, alpha=alpha, fc=fc),
        grid=(t // tm,),
        in_specs=[pl.BlockSpec((tm, d), lambda i: (i, 0))] + [_const_spec(w.shape) for w in consts],
        out_specs=pl.BlockSpec((tm, d), lambda i: (i, 0)),
        out_shape=jax.ShapeDtypeStruct((t, d), F32),
        compiler_params=_params(1),
        name="ffn_ln",
    )(x, *consts)


def _ffn_prep_kernel(x_ref, wg_ref, wu_ref, wd_ref, lg_ref, lb_ref, wqkv_ref, wab_ref, wabt_ref, cw_ref,
                     alog_r_ref, dtb_r_ref, alog_c_ref, dtb_c_ref,
                     x1_ref, q_ref, k_ref, kb_ref, qg_ref, kd_ref, rhs_ref, gc_ref, gl_ref, gcr_ref,
                     ext_ref, *, alpha, fc, tl):
    halo = SUBLANES
    kw = GDN_HEADS * GDN_DK
    nc = tl // CHUNK

    @pl.when(pl.program_id(1) == 0)
    def _():
        ext_ref[0:halo, :] = jnp.zeros((halo, ext_ref.shape[1]), F32)

    y = _ffn_residual(x_ref[...], wg_ref, wu_ref, wd_ref, alpha=alpha, fc=fc)
    x1 = _layer_norm(y, lg_ref[...], lb_ref[...])
    x1_ref[...] = x1
    xb = x1.astype(BF16)
    ext_ref[halo:halo + tl, :] = jnp.dot(xb, wqkv_ref[...], preferred_element_type=F32)
    ab = jnp.dot(xb, wab_ref[...], preferred_element_type=F32)
    arow = lax.dot_general(wabt_ref[...], xb, (((1,), (1,)), ((), ())), preferred_element_type=F32)

    row = lax.broadcasted_iota(jnp.int32, (tl, tl), 0)
    col = lax.broadcasted_iota(jnp.int32, (tl, tl), 1)
    same = (row // CHUNK) == (col // CHUNK)
    tri_b = jnp.where(same & (row >= col), 1.0, 0.0).astype(BF16)
    utri_b = jnp.where(same & (row <= col), 1.0, 0.0).astype(BF16)
    g_col = -jnp.exp(alog_r_ref[...]) * _softplus(ab + dtb_r_ref[...])
    gc_col = sum(jnp.dot(tri_b, part, preferred_element_type=F32) for part in _split3(g_col))
    gl_col = jnp.concatenate(
        [jnp.broadcast_to(gc_col[(c + 1) * CHUNK - 1:(c + 1) * CHUNK, :], (CHUNK, LANES))
         for c in range(nc)], axis=0)
    g_row = -jnp.exp(alog_c_ref[...]) * _softplus(arow + dtb_c_ref[...])
    gcr_ref[...] = sum(jnp.dot(part, utri_b, preferred_element_type=F32) for part in _split3(g_row))
    gc_ref[...] = gc_col
    gl_ref[...] = gl_col
    beta_all = jax.nn.sigmoid(ab)
    e_all = jnp.exp(gc_col)
    d_all = jnp.exp(gl_col - gc_col)

    def conv_silu(c0):
        acc = None
        for j in reversed(range(SHORT_CONV)):
            r0 = halo - (SHORT_CONV - 1) + j
            term = cw_ref[j:j + 1, c0:c0 + LANES] * ext_ref[r0:r0 + tl, c0:c0 + LANES]
            acc = term if acc is None else acc + term
        return _silu(acc)

    for h in range(GDN_HEADS):
        hs = slice(h * GDN_DK, (h + 1) * GDN_DK)
        q = conv_silu(h * GDN_DK)
        k = conv_silu(kw + h * GDN_DK)
        v = conv_silu(2 * kw + h * GDN_DV)
        q = q * (lax.rsqrt(jnp.sum(q * q, axis=-1, keepdims=True) + RMS_EPS) * (GDN_DK ** -0.5))
        k = k * lax.rsqrt(jnp.sum(k * k, axis=-1, keepdims=True) + RMS_EPS)
        beta = beta_all[:, GDN_HEADS + h:GDN_HEADS + h + 1]
        e_g = e_all[:, h:h + 1]
        kb = k * beta
        q_ref[:, hs] = q.astype(BF16)
        k_ref[:, hs] = k.astype(BF16)
        kb_ref[:, hs] = kb.astype(BF16)
        qg_ref[:, hs] = (q * e_g).astype(BF16)
        kd_ref[:, hs] = (k * d_all[:, h:h + 1]).astype(BF16)
        rhs_ref[:, 2 * h * GDN_DV:(2 * h + 1) * GDN_DV] = (v * beta).astype(BF16)
        rhs_ref[:, (2 * h + 1) * GDN_DV:(2 * h + 2) * GDN_DV] = (kb * e_g).astype(BF16)

    ext_ref[0:halo, :] = ext_ref[tl:tl + halo, :]


def _ffn_prep(x, wg, wu, wd, lg, lb, wqkv, wab, wabt, cw, alog_r, dtb_r, alog_c, dtb_c,
              *, alpha, batch, tl=256, fc=2816):
    t, d = x.shape
    nl = t // batch // tl
    kw = GDN_HEADS * GDN_DK
    vw = GDN_HEADS * GDN_DV
    nr = wabt.shape[0]
    row = lambda b, l: (b * nl + l, 0)
    consts = (wg, wu, wd, lg, lb, wqkv, wab, wabt, cw, alog_r, dtb_r, alog_c, dtb_c)
    wide = lambda n, dt: (pl.BlockSpec((tl, n), row), jax.ShapeDtypeStruct((t, n), dt))
    outs = [wide(d, F32)] + [wide(kw, BF16)] * 5 + [wide(2 * vw, BF16), wide(LANES, F32), wide(LANES, F32),
                                                    (pl.BlockSpec((nr, tl), lambda b, l: (0, b * nl + l)),
                                                     jax.ShapeDtypeStruct((nr, t), F32))]
    return pl.pallas_call(
        functools.partial(_ffn_prep_kernel, alpha=alpha, fc=fc, tl=tl),
        grid=(batch, nl),
        in_specs=[pl.BlockSpec((tl, d), row)] + [_const_spec(w.shape) for w in consts],
        out_specs=[o[0] for o in outs],
        out_shape=[o[1] for o in outs],
        scratch_shapes=[pltpu.VMEM((SUBLANES + tl, wqkv.shape[1]), F32)],
        compiler_params=_params(2),
        name="ffn_prep",
    )(x, *consts)


def _conv31_block(w_ref, ext_ref, y_ref, c0, t0, *, rb, anchor=None):
    cols = slice(c0, c0 + LANES)
    acc = None
    for r in range(SUBLANES):
        z = None
        for q in range((CONV_WIDTH - 1 - r) // SUBLANES + 1):
            j = CONV_WIDTH - 1 - (SUBLANES * q + r)
            lo = CONV_HALO + t0 - SUBLANES * (q + 1)
            w = w_ref[j:j + 1, cols]
            if anchor is not None:
                w = jnp.where(anchor[0], w, anchor[1])
            term = w * ext_ref[lo:lo + rb + SUBLANES, cols]
            z = term if z is None else z + term
        sh = z[SUBLANES - r:SUBLANES - r + rb]
        acc = sh if acc is None else acc + sh
    y_ref[t0:t0 + rb, cols] = acc


def _mixer_kernel(x_ref, q_ref, k_ref, kb_ref, qg_ref, kd_ref, rhs_ref, gc_ref, gl_ref, gcr_ref,
                  ng_ref, wz_ref, wglu_ref, wgate_ref, cw_ref, cb_ref, lg_ref, lb_ref, ones_ref,
                  o_ref, cact_ref, sg_ref, s_ref, ext_ref, y_ref, *, tl, rb, plain_blocks, anchor_every):
    nc = tl // CHUNK
    heads = range(GDN_HEADS)
    ch = ext_ref.shape[1]

    @pl.when(pl.program_id(1) == 0)
    def _():
        s_ref[...] = jnp.zeros(s_ref.shape, F32)
        ext_ref[0:CONV_HALO, :] = jnp.zeros((CONV_HALO, ch), F32)

    xb = x_ref[...].astype(BF16)
    for c0 in range(0, ch, LANES):
        glu = jnp.dot(xb, wglu_ref[:, 2 * c0:2 * c0 + 2 * LANES], preferred_element_type=F32)
        ext_ref[CONV_HALO:CONV_HALO + tl, c0:c0 + LANES] = glu[:, :LANES] * jax.nn.sigmoid(glu[:, LANES:])
    blocks = [(c0, t0) for c0 in range(0, ch, LANES) for t0 in range(0, tl, rb)]
    late = blocks[plain_blocks:]
    for c0, t0 in blocks[:plain_blocks]:
        _conv31_block(cw_ref, ext_ref, y_ref, c0, t0, rb=rb)
    always = ones_ref[...] > 0.5
    ticks = [0]

    def tick(val):
        ticks[0] += 1
        if late and ticks[0] % anchor_every == 0:
            c0, t0 = late.pop(0)
            _conv31_block(cw_ref, ext_ref, y_ref, c0, t0, rb=rb, anchor=(always, val[0:1, 0:LANES]))
        return val

    zs = _silu(jnp.dot(xb, wz_ref[...], preferred_element_type=F32))
    sg_ref[...] = jax.nn.sigmoid(
        jnp.dot(xb, wgate_ref[...], preferred_element_type=F32)).astype(sg_ref.dtype)

    row = lax.broadcasted_iota(jnp.int32, (tl, tl), 0)
    col = lax.broadcasted_iota(jnp.int32, (tl, tl), 1)
    same_b = jnp.where((row // CHUNK) == (col // CHUNK), 1.0, 0.0).astype(BF16)
    pi = lax.broadcasted_iota(jnp.int32, (CHUNK, tl), 0)
    lane = lax.broadcasted_iota(jnp.int32, (CHUNK, tl), 1)
    lc = lane // CHUNK
    lj = lane % CHUNK
    tri_p = pi >= lj
    strict_p = pi > lj
    eye_p = jnp.where(pi == lj, 1.0, 0.0)
    base = 16
    base_p = (pi // base) == (lj // base)

    def pack(full):
        p = full[0:CHUNK]
        for c in range(1, nc):
            p = jnp.where(lc == c, full[c * CHUNK:(c + 1) * CHUNK], p)
        return p

    def blockdiag(p):
        return jnp.concatenate([p.astype(BF16)] * nc, axis=0) * same_b

    def pmm(a, b):
        return jnp.dot(a.astype(BF16), blockdiag(b), preferred_element_type=F32)

    m, a = [], []
    for h in heads:
        hs = slice(h * GDN_DK, (h + 1) * GDN_DK)
        k = k_ref[:, hs]
        g_pack = pack(jnp.broadcast_to(gc_ref[:, h:h + 1], (tl, tl)))
        diff = g_pack - gcr_ref[h:h + 1, :]
        decay = jnp.where(tri_p, jnp.exp(jnp.where(tri_p, diff, 0.0)), 0.0)
        m.append(jnp.where(strict_p, pack(_mm_nt(kb_ref[:, hs], k)) * decay, 0.0))
        a.append(tick(pack(_mm_nt(q_ref[:, hs], k)) * decay))

    n = [jnp.where(base_p, -m[h], 0.0) for h in heads]
    inv = [eye_p + n[h] for h in heads]
    for _ in range(3):
        n = [tick(pmm(n[h], n[h])) for h in heads]
        inv = [tick(inv[h] + pmm(n[h], inv[h])) for h in heads]
    s = base
    while s < CHUNK:
        lower_left = ((pi // (2 * s)) == (lj // (2 * s))) & ((pi // s) == (lj // s) + 1)
        e = [tick(pmm(jnp.where(lower_left, m[h], 0.0), inv[h])) for h in heads]
        inv = [tick(inv[h] - pmm(inv[h], e[h])) for h in heads]
        s *= 2
    xb16 = [jnp.dot(blockdiag(inv[h]), rhs_ref[:, 2 * h * GDN_DV:(2 * h + 2) * GDN_DV],
                    preferred_element_type=F32).astype(BF16) for h in heads]

    ax = [tick(jnp.dot(blockdiag(a[h]), xb16[h], preferred_element_type=F32)) for h in heads]
    kx = [[lax.dot_general(kd_ref[c * CHUNK:(c + 1) * CHUNK, h * GDN_DK:(h + 1) * GDN_DK],
                           xb16[h][c * CHUNK:(c + 1) * CHUNK], (((0,), (0,)), ((), ())),
                           preferred_element_type=F32) for c in range(nc)] for h in heads]
    state = [s_ref[h] for h in heads]
    outs = [[] for _ in heads]
    for c in range(nc):
        r = slice(c * CHUNK, (c + 1) * CHUNK)
        for h in heads:
            hs = slice(h * GDN_DK, (h + 1) * GDN_DK)
            sb = state[h].astype(BF16)
            r_c = (qg_ref[r, hs].astype(F32) - ax[h][r, GDN_DV:]).astype(BF16)
            outs[h].append(jnp.dot(r_c, sb, preferred_element_type=F32) + ax[h][r, :GDN_DV])
            state[h] = (state[h] * jnp.exp(gl_ref[c * CHUNK:c * CHUNK + 1, h:h + 1])
                        - jnp.dot(kx[h][c][:, GDN_DV:].astype(BF16), sb, preferred_element_type=F32)
                        + kx[h][c][:, :GDN_DV])
            tick(state[h])

    for c0, t0 in late:
        _conv31_block(cw_ref, ext_ref, y_ref, c0, t0, rb=rb)
    cact_ref[...] = _silu(_layer_norm(y_ref[...] + cb_ref[...], lg_ref[...], lb_ref[...])
                          ).astype(cact_ref.dtype)
    ext_ref[0:CONV_HALO, :] = ext_ref[tl:tl + CONV_HALO, :]
    for h in heads:
        hs = slice(h * GDN_DV, (h + 1) * GDN_DV)
        s_ref[h] = state[h]
        o = jnp.concatenate(outs[h], axis=0)
        o = o * lax.rsqrt(jnp.mean(o * o, axis=-1, keepdims=True) + RMS_EPS) * ng_ref[...]
        o_ref[:, hs] = (o * zs[:, hs]).astype(o_ref.dtype)


def _mixer(x, q, k, kb, qg, kd, rhs, gc, gl, gcr, ng, wz, wglu, wgate, cw, cb, lg, lb,
           *, batch, tl=256, rb=64, plain_blocks=20, anchor_every=8):
    t = x.shape[0]
    nl = t // batch // tl
    vw, ch, ns = wz.shape[1], wglu.shape[1] // 2, wgate.shape[1]
    row = lambda b, l: (b * nl + l, 0)
    tiles = (x, q, k, kb, qg, kd, rhs, gc, gl)
    consts = (ng, wz, wglu, wgate, cw, cb, lg, lb, jnp.ones((1, LANES), F32))
    return pl.pallas_call(
        functools.partial(_mixer_kernel, tl=tl, rb=rb, plain_blocks=plain_blocks,
                          anchor_every=anchor_every),
        grid=(batch, nl),
        in_specs=([pl.BlockSpec((tl, v.shape[1]), row) for v in tiles]
                  + [pl.BlockSpec((gcr.shape[0], tl), lambda b, l: (0, b * nl + l))]
                  + [_const_spec(w.shape) for w in consts]),
        out_specs=[pl.BlockSpec((tl, vw), row), pl.BlockSpec((tl, ch), row), pl.BlockSpec((tl, ns), row)],
        out_shape=[jax.ShapeDtypeStruct((t, vw), BF16), jax.ShapeDtypeStruct((t, ch), BF16),
                   jax.ShapeDtypeStruct((t, ns), BF16)],
        scratch_shapes=[pltpu.VMEM((GDN_HEADS, GDN_DK, GDN_DV), F32),
                        pltpu.VMEM((CONV_HALO + tl, ch), F32), pltpu.VMEM((tl, ch), F32)],
        compiler_params=_params(2),
        name="mixer",
    )(*tiles, gcr, *consts)


def _merge_kernel(x_ref, o_ref, c_ref, sg_ref, wa_ref, wc_ref, bc_ref, wm_ref, g_ref, b_ref,
                  out_ref, *, alpha, groups):
    d = x_ref.shape[1]
    wa, wc, wm = (w[...].astype(BF16) for w in (wa_ref, wc_ref, wm_ref))
    rows = x_ref.shape[0] // groups
    for i in range(groups):
        r = slice(i * rows, (i + 1) * rows)
        y_a = jnp.dot(o_ref[r, :], wa, preferred_element_type=F32)
        y_c = jnp.dot(c_ref[r, :], wc, preferred_element_type=F32) + bc_ref[...]
        y = sg_ref[r, :d] * y_a + sg_ref[r, d:] * y_c
        mixed = jnp.dot(y.astype(BF16), wm, preferred_element_type=F32)
        out_ref[r, :] = _layer_norm(alpha * x_ref[r, :] + mixed, g_ref[...], b_ref[...])


def _merge(x, o, c, sg, wa, wc, bc, wm, g, b, *, alpha, tm=1024, groups=4):
    t, d = x.shape
    row = lambda i: (i, 0)
    return pl.pallas_call(
        functools.partial(_merge_kernel, alpha=alpha, groups=groups),
        grid=(t // tm,),
        in_specs=[pl.BlockSpec((tm, d), row), pl.BlockSpec((tm, o.shape[1]), row),
                  pl.BlockSpec((tm, c.shape[1]), row), pl.BlockSpec((tm, sg.shape[1]), row),
                  _const_spec(wa.shape), _const_spec(wc.shape), _const_spec(bc.shape),
                  _const_spec(wm.shape), _const_spec(g.shape), _const_spec(b.shape)],
        out_specs=pl.BlockSpec((tm, d), row),
        out_shape=jax.ShapeDtypeStruct((t, d), F32),
        compiler_params=_params(1),
        name="merge",
    )(x, o, c, sg, wa, wc, bc, wm, g, b)


def _xattn_kernel(x_ref, mem_ref, wkv_ref, wq_ref, wo_ref, g_ref, b_ref, out_ref, kv_ref, *, alpha, groups):
    d = x_ref.shape[1]
    hd = d // XATTN_HEADS

    @pl.when(pl.program_id(1) == 0)
    def _():
        kv_ref[...] = jnp.dot(mem_ref[...].astype(BF16), wkv_ref[...].astype(BF16),
                              preferred_element_type=F32).astype(kv_ref.dtype)

    wq, wo = wq_ref[...].astype(BF16), wo_ref[...].astype(BF16)
    rows = x_ref.shape[0] // groups
    for i in range(groups):
        r = slice(i * rows, (i + 1) * rows)
        x = x_ref[r, :]
        q = jnp.dot(x.astype(BF16), wq, preferred_element_type=F32).astype(BF16)
        heads = []
        for h in range(XATTN_HEADS):
            k = kv_ref[:, h * hd:(h + 1) * hd]
            v = kv_ref[:, d + h * hd:d + (h + 1) * hd]
            s = _mm_nt(q[:, h * hd:(h + 1) * hd], k) * (hd ** -0.5)
            e = jnp.exp(s - jnp.max(s, axis=-1, keepdims=True))
            p = e / jnp.sum(e, axis=-1, keepdims=True)
            heads.append(jnp.dot(p.astype(BF16), v, preferred_element_type=F32).astype(BF16))
        o = jnp.concatenate(heads, axis=1)
        att = jnp.dot(o, wo, preferred_element_type=F32)
        out_ref[r, :] = _layer_norm(alpha * x + att, g_ref[...], b_ref[...])


def _xattn(x, mem, wkv, wq, wo, g, b, *, alpha, batch, tl=1024, groups=2):
    t, d = x.shape
    nl = t // batch // tl
    n_mem = mem.shape[0] // batch
    row = lambda bi, l: (bi * nl + l, 0)
    return pl.pallas_call(
        functools.partial(_xattn_kernel, alpha=alpha, groups=groups),
        grid=(batch, nl),
        in_specs=[pl.BlockSpec((tl, d), row),
                  pl.BlockSpec((n_mem, d), lambda bi, l: (bi, 0)),
                  _const_spec(wkv.shape), _const_spec(wq.shape), _const_spec(wo.shape),
                  _const_spec(g.shape), _const_spec(b.shape)],
        out_specs=pl.BlockSpec((tl, d), row),
        out_shape=jax.ShapeDtypeStruct((t, d), F32),
        scratch_shapes=[pltpu.VMEM((n_mem, wkv.shape[1]), BF16)],
        compiler_params=_params(2),
        name="xattn",
    )(x, mem, wkv, wq, wo, g, b)


def _row(v):
    return v.reshape(1, -1).astype(F32)


def _pad_to(v, shape):
    return jnp.pad(v, tuple((0, n - s) for s, n in zip(v.shape, shape)))


def _interleave_halves(w):
    k, n = w.shape
    return w.reshape(k, 2, n // 2 // LANES, LANES).transpose(0, 2, 1, 3).reshape(k, n)


def _layer(x, mem, p, *, batch, alpha):
    d = x.shape[1]
    kw = GDN_HEADS * GDN_DK
    vw = GDN_HEADS * GDN_DV
    w_in = p["w_in"]
    o0 = 2 * kw + vw
    o1 = o0 + vw
    o2 = o1 + 2 * GDN_HEADS
    o3 = o2 + 2 * d
    w_ab = w_in[:, o1:o2]
    wab = _pad_to(w_ab, (d, LANES)).astype(BF16)
    wabt = _pad_to(w_ab[:, :GDN_HEADS].T, (BF16_ROWS, d)).astype(BF16)
    alog = _row(p["gdn_a_log"])
    dtb = _row(p["gdn_dt_bias"])

    x, q, k, kb, qg, kd, rhs, gc, gl, gcr = _ffn_prep(
        x, p["ffn1_wg"].astype(BF16), p["ffn1_wu"].astype(BF16), p["ffn1_wd"].astype(BF16),
        _row(p["ln1_g"]), _row(p["ln1_b"]), w_in[:, :o0].astype(BF16), wab, wabt,
        p["gdn_conv_qkv"].astype(F32), _pad_to(alog, (1, LANES)), _pad_to(dtb, (1, LANES)),
        _pad_to(alog.reshape(-1, 1), (BF16_ROWS, 1)), _pad_to(dtb.reshape(-1, 1), (BF16_ROWS, 1)),
        alpha=alpha, batch=batch)
    o, c, sg = _mixer(
        x, q, k, kb, qg, kd, rhs, gc, gl, gcr, _row(p["gdn_norm_g"]),
        w_in[:, o0:o1].astype(BF16), _interleave_halves(w_in[:, o2:o3]).astype(BF16),
        w_in[:, o3:].astype(BF16),
        p["conv_dw_w"].astype(F32), _row(p["conv_dw_b"]), _row(p["conv_ln_g"]), _row(p["conv_ln_b"]),
        batch=batch)
    x = _merge(x, o, c, sg, p["w_gdn_out"], p["w_conv_out"], _row(p["b_conv_out"]), p["w_mix_out"],
               _row(p["ln2_g"]), _row(p["ln2_b"]), alpha=alpha)

    x = _xattn(x, mem, p["w_xkv"], p["w_xq"], p["w_xo"], _row(p["ln3_g"]), _row(p["ln3_b"]),
               alpha=alpha, batch=batch)

    return _ffn_ln(x, p["ffn2_wg"], p["ffn2_wu"], p["ffn2_wd"],
                   _row(p["ln4_g"]), _row(p["ln4_b"]), alpha=alpha)


def kernel(x, mem, ffn1_wg, ffn1_wu, ffn1_wd, ln1_g, ln1_b, w_in, gdn_conv_qkv, gdn_a_log, gdn_dt_bias, gdn_norm_g, w_gdn_out, conv_dw_w, conv_dw_b, conv_ln_g, conv_ln_b, w_conv_out, b_conv_out, w_mix_out, ln2_g, ln2_b, w_xq, w_xkv, w_xo, ln3_g, ln3_b, ffn2_wg, ffn2_wu, ffn2_wd, ln4_g, ln4_b):
    weights = dict(
        ffn1_wg=ffn1_wg, ffn1_wu=ffn1_wu, ffn1_wd=ffn1_wd, ln1_g=ln1_g, ln1_b=ln1_b, w_in=w_in,
        gdn_conv_qkv=gdn_conv_qkv, gdn_a_log=gdn_a_log, gdn_dt_bias=gdn_dt_bias,
        gdn_norm_g=gdn_norm_g, w_gdn_out=w_gdn_out, conv_dw_w=conv_dw_w, conv_dw_b=conv_dw_b,
        conv_ln_g=conv_ln_g, conv_ln_b=conv_ln_b, w_conv_out=w_conv_out, b_conv_out=b_conv_out,
        w_mix_out=w_mix_out, ln2_g=ln2_g, ln2_b=ln2_b, w_xq=w_xq, w_xkv=w_xkv, w_xo=w_xo,
        ln3_g=ln3_g, ln3_b=ln3_b, ffn2_wg=ffn2_wg, ffn2_wu=ffn2_wu, ffn2_wd=ffn2_wd,
        ln4_g=ln4_g, ln4_b=ln4_b)
    batch, seq, d = x.shape
    depth = ffn1_wg.shape[0]
    alpha = (2 * depth) ** 0.25
    h = x.reshape(batch * seq, d)
    m = mem.reshape(-1, d)
    for i in range(depth):
        h = _layer(h, m, {k: v[i] for k, v in weights.items()}, batch=batch, alpha=alpha)
    return h.reshape(batch, seq, d)
```
